```python
import jax, jax.numpy as jnp
from jax import lax
import numpy as np

D_MODEL = 1024
BATCH = 2
SEQ = 8192
DEPTH = 1

N_HEADS = 8
HEAD_DIM = 64
ATTN_WIDTH = N_HEADS * HEAD_DIM
IDX_HEADS = 8
IDX_DIM = 64
TOPK_MAX = 256
Q_BLOCK = 128
CONV_CH = 512
CONV_WIDTH = 31
D_FF = -(-8 * D_MODEL // (3 * 256)) * 256
N_MOD = 6
EPS = 1e-6

kernel_name = "hybrid_dsa_conformer_gated_block"


def _in_split_sizes():
    return [ATTN_WIDTH, ATTN_WIDTH, ATTN_WIDTH,
            IDX_HEADS * IDX_DIM, IDX_DIM, IDX_HEADS,
            2 * CONV_CH, D_MODEL, D_MODEL]


def rmsnorm(x, g):
    xf = x.astype(jnp.float32)
    y = xf * lax.rsqrt(jnp.mean(xf * xf, axis=-1, keepdims=True) + EPS) * g.astype(jnp.float32)
    return y.astype(x.dtype)


def layernorm(x, g, b):
    xf = x.astype(jnp.float32)
    mu = jnp.mean(xf, axis=-1, keepdims=True)
    var = jnp.mean(jnp.square(xf - mu), axis=-1, keepdims=True)
    y = (xf - mu) * lax.rsqrt(var + EPS) * g.astype(jnp.float32) + b.astype(jnp.float32)
    return y.astype(x.dtype)


def modulate(h, shift, scale):
    return h * (1.0 + scale[:, None, :]) + shift[:, None, :]


def dsa_attention(q, k, v, q_idx, k_idx, w_idx):
    B, S = q.shape[0], q.shape[1]
    topk = min(TOPK_MAX, S // 4)
    n_blocks = S // Q_BLOCK
    f32 = jnp.float32
    slopes = jnp.exp2(-8.0 * jnp.arange(1, N_HEADS + 1, dtype=f32) / N_HEADS)
    key_pos = jnp.arange(S)
    k_idx32 = k_idx.astype(f32)
    gather = jax.vmap(lambda arr, ix: arr[ix])

    def one_block(i):
        start = i * Q_BLOCK
        qb = lax.dynamic_slice_in_dim(q, start, Q_BLOCK, axis=1).astype(f32)
        qib = lax.dynamic_slice_in_dim(q_idx, start, Q_BLOCK, axis=1).astype(f32)
        wb = lax.dynamic_slice_in_dim(w_idx, start, Q_BLOCK, axis=1).astype(f32)
        q_pos = start + jnp.arange(Q_BLOCK)
        idx_logits = jnp.einsum('btjd,bsd->btjs', qib, k_idx32) * (IDX_DIM ** -0.5)
        score = jnp.einsum('btj,btjs->bts', wb, jax.nn.relu(idx_logits))
        causal = key_pos[None, :] <= q_pos[:, None]
        score = jnp.where(causal[None], score, -jnp.inf)
        _, sel = lax.top_k(score, topk)
        k_sel = gather(k, sel).astype(f32)
        v_sel = gather(v, sel).astype(f32)
        s = jnp.einsum('bthd,btkhd->bhtk', qb, k_sel) * (HEAD_DIM ** -0.5)
        dist = (q_pos[None, :, None] - sel).astype(f32)
        s = s - slopes[None, :, None, None] * dist[:, None]
        valid = sel <= q_pos[None, :, None]
        s = jnp.where(valid[:, None], s, -jnp.inf)
        p = jax.nn.softmax(s, axis=-1)
        out = jnp.einsum('bhtk,btkhd->bthd', p, v_sel)
        return out.astype(q.dtype)

    outs = lax.map(one_block, jnp.arange(n_blocks))
    return jnp.transpose(outs, (1, 0, 2, 3, 4)).reshape(B, S, N_HEADS * HEAD_DIM)


def conformer_conv(u, w_dw, b_dw, ln_g, ln_b):
    a, g = jnp.split(u, 2, axis=-1)
    z = a * jax.nn.sigmoid(g)
    z = lax.conv_general_dilated(z, w_dw, window_strides=(1,), padding=[(CONV_WIDTH - 1, 0)],
                                 dimension_numbers=('NWC', 'WIO', 'NWC'),
                                 feature_group_count=CONV_CH) + b_dw
    z = layernorm(z, ln_g, ln_b)
    return jax.nn.silu(z)


def setup_inputs(seed: int = 0) -> dict:
    key = jax.random.key(seed)
    ks = jax.random.split(key, 20)
    n_in = sum(_in_split_sizes())

    def nrm(k, shape, fan_in, gain=1.0):
        return jax.random.normal(k, shape, jnp.float32) * (gain * fan_in ** -0.5)

    def gain_vec(k, shape):
        return 1.0 + 0.02 * jax.random.normal(k, shape, jnp.float32)

    def small(k, shape):
        return 0.02 * jax.random.normal(k, shape, jnp.float32)

    return {
        "x": jax.random.normal(ks[0], (BATCH, SEQ, D_MODEL), jnp.float32),
        "c": jax.random.normal(ks[1], (BATCH, D_MODEL), jnp.float32),
        "norm_mix_g": gain_vec(ks[2], (DEPTH, D_MODEL)),
        "w_in": nrm(ks[3], (DEPTH, D_MODEL, n_in), D_MODEL),
        "w_dw": nrm(ks[4], (DEPTH, CONV_WIDTH, 1, CONV_CH), CONV_WIDTH),
        "b_dw": small(ks[5], (DEPTH, CONV_CH)),
        "conv_ln_g": gain_vec(ks[6], (DEPTH, CONV_CH)),
        "conv_ln_b": small(ks[7], (DEPTH, CONV_CH)),
        "w_attn_proj": nrm(ks[8], (DEPTH, ATTN_WIDTH, D_MODEL), ATTN_WIDTH),
        "w_conv_proj": nrm(ks[9], (DEPTH, CONV_CH, D_MODEL), CONV_CH),
        "w_out": nrm(ks[10], (DEPTH, D_MODEL, D_MODEL), D_MODEL),
        "norm_ffn_g": gain_vec(ks[11], (DEPTH, D_MODEL)),
        "w_ffn_in": nrm(ks[12], (DEPTH, D_MODEL, 2 * D_FF), D_MODEL),
        "w_ffn_out": nrm(ks[13], (DEPTH, D_FF, D_MODEL), D_FF),
        "w_ada": nrm(ks[14], (DEPTH, D_MODEL, N_MOD * D_MODEL), D_MODEL, 0.5),
        "b_ada": small(ks[15], (DEPTH, N_MOD * D_MODEL)),
        "norm_final_g": gain_vec(ks[16], (D_MODEL,)),
    }


def reference(x, c, norm_mix_g, w_in, w_dw, b_dw, conv_ln_g, conv_ln_b, w_attn_proj,
              w_conv_proj, w_out, norm_ffn_g, w_ffn_in, w_ffn_out, w_ada, b_ada, norm_final_g):
    B, S, _ = x.shape
    split_at = list(np.cumsum(_in_split_sizes())[:-1])
    c_act = jax.nn.silu(c)
    for l in range(DEPTH):
        mod = c_act @ w_ada[l] + b_ada[l]
        sh_m, sc_m, g_m, sh_f, sc_f, g_f = jnp.split(mod, N_MOD, axis=-1)

        h = modulate(rmsnorm(x, norm_mix_g[l]), sh_m, sc_m)
        proj = h @ w_in[l]
        q, k, v, qi, ki, wi, u, ga, gb = jnp.split(proj, split_at, axis=-1)
        q = q.reshape(B, S, N_HEADS, HEAD_DIM)
        k = k.reshape(B, S, N_HEADS, HEAD_DIM)
        v = v.reshape(B, S, N_HEADS, HEAD_DIM)
        qi = qi.reshape(B, S, IDX_HEADS, IDX_DIM)
        wi = wi * (IDX_HEADS ** -0.5)
        y_attn = dsa_attention(q, k, v, qi, ki, wi) @ w_attn_proj[l]
        y_conv = conformer_conv(u, w_dw[l], b_dw[l], conv_ln_g[l], conv_ln_b[l]) @ w_conv_proj[l]
        merged = jax.nn.sigmoid(ga) * y_attn + jax.nn.sigmoid(gb) * y_conv
        x = x + g_m[:, None, :] * (merged @ w_out[l])

        h = modulate(rmsnorm(x, norm_ffn_g[l]), sh_f, sc_f)
        a, b = jnp.split(h @ w_ffn_in[l], 2, axis=-1)
        x = x + g_f[:, None, :] * ((jax.nn.silu(a) * b) @ w_ffn_out[l])
    return rmsnorm(x, norm_final_g)
```

```python
import functools

import jax
import jax.numpy as jnp
from jax import lax
from jax.experimental import pallas as pl
from jax.experimental.pallas import tpu as pltpu

N_HEADS = 8
HEAD_DIM = 64
ATTN_WIDTH = N_HEADS * HEAD_DIM
IDX_HEADS = 8
IDX_DIM = 64
TOPK_MAX = 256
CONV_CH = 512
CONV_WIDTH = 31
N_MOD = 6
EPS = 1e-6

LANES = 128
CONV_HALO = 32
NEG_BIG = -1e30
VMEM_LIMIT = 56 * 1024 * 1024

f32 = jnp.float32
bf16 = jnp.bfloat16


def _resident(block_shape, index_map):
    return pl.BlockSpec(block_shape, index_map, pipeline_mode=pl.Buffered(1))


def _ada_kernel(c_ref, w_ref, b_ref, o_ref):
    c = c_ref[...]
    ca = c * jax.nn.sigmoid(c)
    o_ref[...] = jnp.dot(ca, w_ref[...], preferred_element_type=f32,
                         precision=lax.Precision.HIGHEST) + b_ref[...]


def _ada(c, w_ada, b_ada):
    B, D = c.shape
    N = w_ada.shape[1]
    rows = 8
    c_pad = jnp.zeros((rows, D), f32).at[:B].set(c)
    tn = 1024
    out = pl.pallas_call(
        _ada_kernel,
        out_shape=jax.ShapeDtypeStruct((rows, N), f32),
        grid=(N // tn,),
        in_specs=[pl.BlockSpec((rows, D), lambda j: (0, 0)),
                  pl.BlockSpec((D, tn), lambda j: (0, j)),
                  pl.BlockSpec((1, tn), lambda j: (0, j))],
        out_specs=pl.BlockSpec((rows, tn), lambda j: (0, j)),
        compiler_params=pltpu.CompilerParams(dimension_semantics=("arbitrary",)),
        name="ada",
    )(c_pad, w_ada, b_ada.reshape(1, N))
    return out[:B]


def _inproj_kernel(x_ref, sh_ref, sc_ref, g_ref, wt_ref, wk_ref, wki_ref, wwi_ref,
                   wua_ref, wug_ref, wga_ref, wgb_ref,
                   qT_ref, qiT_ref, vT_ref, k_ref, ki_ref, wT_ref, z_ref, sga_ref, sgb_ref,
                   *, tk):
    x = x_ref[0]
    ms = jnp.mean(x * x, axis=-1, keepdims=True)
    h = x * lax.rsqrt(ms + EPS) * g_ref[...]
    h = h * (1.0 + sc_ref[0]) + sh_ref[0]
    hb = h.astype(bf16)

    def proj(w_ref):
        return jnp.dot(hb, w_ref[...], preferred_element_type=f32)

    ptT = proj(wt_ref).T
    qT_ref[0] = ptT[0:ATTN_WIDTH].astype(bf16)
    qiT_ref[0] = ptT[ATTN_WIDTH:2 * ATTN_WIDTH].astype(bf16)
    tm = x.shape[0]
    for t in range(tm // tk):
        vT_ref[0, t] = ptT[2 * ATTN_WIDTH:3 * ATTN_WIDTH, t * tk:(t + 1) * tk].astype(bf16)
    k_ref[0] = proj(wk_ref).astype(bf16)
    ki_ref[0] = proj(wki_ref).astype(bf16)
    wiT = proj(wwi_ref).T
    wT_ref[0] = wiT[0:IDX_HEADS] * (IDX_HEADS ** -0.5)
    z_ref[0] = proj(wua_ref) * jax.nn.sigmoid(proj(wug_ref))
    sga_ref[0] = jax.nn.sigmoid(proj(wga_ref)).astype(bf16)
    sgb_ref[0] = jax.nn.sigmoid(proj(wgb_ref)).astype(bf16)


def _inproj(x, shift, scale, g, w_in, *, tm, tk):
    B, S, D = x.shape
    W = ATTN_WIDTH
    o = 0
    wq = w_in[:, o:o + W]; o += W
    wk = w_in[:, o:o + W]; o += W
    wv = w_in[:, o:o + W]; o += W
    wqi = w_in[:, o:o + IDX_HEADS * IDX_DIM]; o += IDX_HEADS * IDX_DIM
    wki = w_in[:, o:o + IDX_DIM]; o += IDX_DIM
    wwi = w_in[:, o:o + IDX_HEADS]; o += IDX_HEADS
    wua = w_in[:, o:o + CONV_CH]; o += CONV_CH
    wug = w_in[:, o:o + CONV_CH]; o += CONV_CH
    wga = w_in[:, o:o + D]; o += D
    wgb = w_in[:, o:o + D]; o += D
    wt = jnp.concatenate([wq * (HEAD_DIM ** -0.5), wqi * (IDX_DIM ** -0.5), wv], axis=1).astype(bf16)
    pad = lambda w: jnp.pad(w, ((0, 0), (0, LANES - w.shape[1]))).astype(bf16)
    weights = [wt, wk.astype(bf16), pad(wki), pad(wwi), wua.astype(bf16), wug.astype(bf16),
               wga.astype(bf16), wgb.astype(bf16)]

    tile = lambda n: pl.BlockSpec((1, tm, n), lambda b, i: (b, i, 0))
    tileT = lambda n: pl.BlockSpec((1, n, tm), lambda b, i: (b, 0, i))
    vec = pl.BlockSpec((1, 1, D), lambda b, i: (b, 0, 0))
    out_shape = [
        jax.ShapeDtypeStruct((B, W, S), bf16),
        jax.ShapeDtypeStruct((B, W, S), bf16),
        jax.ShapeDtypeStruct((B, S // tk, W, tk), bf16),
        jax.ShapeDtypeStruct((B, S, W), bf16),
        jax.ShapeDtypeStruct((B, S, LANES), bf16),
        jax.ShapeDtypeStruct((B, IDX_HEADS, S), f32),
        jax.ShapeDtypeStruct((B, S, CONV_CH), f32),
        jax.ShapeDtypeStruct((B, S, D), bf16),
        jax.ShapeDtypeStruct((B, S, D), bf16),
    ]
    out_specs = [tileT(W), tileT(W),
                 pl.BlockSpec((1, tm // tk, W, tk), lambda b, i: (b, i, 0, 0)),
                 tile(W), tile(LANES), tileT(IDX_HEADS), tile(CONV_CH), tile(D), tile(D)]
    in_specs = [tile(D), vec, vec, pl.BlockSpec((1, D), lambda b, i: (0, 0))]
    in_specs += [_resident(w.shape, lambda b, i: (0, 0)) for w in weights]
    return pl.pallas_call(
        functools.partial(_inproj_kernel, tk=tk),
        out_shape=out_shape,
        grid=(B, S // tm),
        in_specs=in_specs,
        out_specs=out_specs,
        compiler_params=pltpu.CompilerParams(dimension_semantics=("arbitrary", "arbitrary"),
                                             vmem_limit_bytes=VMEM_LIMIT),
        name="inproj",
    )(x, shift, scale, g.reshape(1, D), *weights)


def _ordered_bits_to_float(u):
    key = u ^ jnp.int32(-2 ** 31)
    bits = key ^ ((key >> 31) & jnp.int32(0x7FFFFFFF))
    return lax.bitcast_convert_type(bits, f32)


def _attn_kernel(qT_ref, qiT_ref, wT_ref, k_ref, vT_ref, ki_ref, o_ref,
                 sc_ref, qpad_ref, qipad_ref, m_ref, l_ref, acc_ref, *, topk):
    tq = qT_ref.shape[2]
    tk = tq
    i = pl.program_id(1)
    q0 = i * tq
    n_tiles = i + 1

    row = lax.broadcasted_iota(jnp.int32, (tk, tq), 0)
    lane = lax.broadcasted_iota(jnp.int32, (tk, tq), 1)
    lane1 = lax.broadcasted_iota(jnp.int32, (1, tq), 1)

    qT = qT_ref[0]
    qiT = qiT_ref[0]
    first_half = lax.broadcasted_iota(jnp.int32, (2 * HEAD_DIM, tq), 0) < HEAD_DIM
    zeros_half = jnp.zeros((IDX_DIM, tq), bf16)
    for h in range(N_HEADS):
        pair = qT[(h // 2) * 2 * HEAD_DIM:(h // 2 + 1) * 2 * HEAD_DIM]
        keep = first_half if h % 2 == 0 else jnp.logical_not(first_half)
        qpad_ref[h] = jnp.where(keep, pair, jnp.zeros_like(pair))
    for h in range(IDX_HEADS):
        qipad_ref[h] = jnp.concatenate([qiT[h * IDX_DIM:(h + 1) * IDX_DIM], zeros_half], axis=0)

    w_rows = wT_ref[0]

    def score_tile(j, carry):
        k0 = pl.multiple_of(j * tk, tk)
        kit = ki_ref[0, pl.ds(k0, tk), :]
        acc = jnp.zeros((tk, tq), f32)
        for h in range(IDX_HEADS):
            lg = jnp.dot(kit, qipad_ref[h], preferred_element_type=f32)
            acc = acc + w_rows[h:h + 1, :] * jnp.maximum(lg, 0.0)
        causal = (k0 + row) <= (q0 + lane)
        sc_ref[pl.ds(k0, tk), :] = jnp.where(causal, acc, -jnp.inf)
        return carry

    lax.fori_loop(0, n_tiles, score_tile, 0)

    kf = jnp.minimum(topk, q0 + lane1 + 1).astype(f32)

    def count(pred):
        def body(j, acc):
            k0 = pl.multiple_of(j * tk, tk)
            s = sc_ref[pl.ds(k0, tk), :]
            hit = jnp.where(pred(s, k0 + row), 1.0, 0.0)
            return acc + hit.reshape(tk // 8, 8, tq).sum(axis=0)
        acc = lax.fori_loop(0, n_tiles, body, jnp.zeros((8, tq), f32))
        return acc.sum(axis=0, keepdims=True)

    def bisect_threshold(it, u):
        cand = u | lax.shift_left(jnp.int32(1), 31 - it)
        thr = _ordered_bits_to_float(cand)
        c = count(lambda s, kpos: s >= thr)
        return jnp.where(c >= kf, cand, u)

    u = lax.fori_loop(0, 32, bisect_threshold, jnp.zeros((1, tq), jnp.int32))
    thr = _ordered_bits_to_float(u)
    thr_next = _ordered_bits_to_float(u + 1)
    n_ge = count(lambda s, kpos: s >= thr)
    excess = jnp.max(n_ge - kf)

    @pl.when(excess > 0.0)
    def _():
        need = kf - count(lambda s, kpos: s >= thr_next)

        def bisect_index(it, jcut):
            cand = jcut | lax.shift_left(jnp.int32(1), 13 - it)
            c = count(lambda s, kpos: (s >= thr) & (s < thr_next) & (kpos < cand))
            return jnp.where(c <= need, cand, jcut)

        jcut = lax.fori_loop(0, 14, bisect_index, jnp.zeros((1, tq), jnp.int32))

        def drop_tile(j, carry):
            k0 = pl.multiple_of(j * tk, tk)
            s = sc_ref[pl.ds(k0, tk), :]
            drop = (s >= thr) & (s < thr_next) & ((k0 + row) >= jcut)
            sc_ref[pl.ds(k0, tk), :] = jnp.where(drop, -jnp.inf, s)
            return carry

        lax.fori_loop(0, n_tiles, drop_tile, 0)

    m_ref[...] = jnp.full(m_ref.shape, NEG_BIG, f32)
    l_ref[...] = jnp.zeros(l_ref.shape, f32)
    acc_ref[...] = jnp.zeros(acc_ref.shape, f32)

    def attn_tile(j, carry):
        k0 = pl.multiple_of(j * tk, tk)
        sel = sc_ref[pl.ds(k0, tk), :] >= thr
        relk = (k0 - q0 + row).astype(f32)
        for h in range(N_HEADS):
            slope = 2.0 ** (-8.0 * (h + 1) / N_HEADS)
            kp = k_ref[0, pl.ds(k0, tk), (h // 2) * LANES:(h // 2 + 1) * LANES]
            s = jnp.dot(kp, qpad_ref[h], preferred_element_type=f32)
            s = jnp.where(sel, s + slope * relk, NEG_BIG)
            m_old = m_ref[h:h + 1, :]
            m_new = jnp.maximum(m_old, s.max(axis=0, keepdims=True))
            alpha = jnp.exp(m_old - m_new)
            p = jnp.exp(s - m_new)
            l_ref[h:h + 1, :] = alpha * l_ref[h:h + 1, :] + p.sum(axis=0, keepdims=True)
            vt = vT_ref[0, j, h * HEAD_DIM:(h + 1) * HEAD_DIM, :]
            pv = jnp.dot(vt, p.astype(bf16), preferred_element_type=f32)
            rows = slice(h * HEAD_DIM, (h + 1) * HEAD_DIM)
            acc_ref[rows, :] = alpha * acc_ref[rows, :] + pv
            m_ref[h:h + 1, :] = m_new
        return carry

    lax.fori_loop(0, n_tiles, attn_tile, 0)

    outs = []
    for h in range(N_HEADS):
        outs.append(acc_ref[h * HEAD_DIM:(h + 1) * HEAD_DIM, :] / l_ref[h:h + 1, :])
    o_ref[0] = jnp.concatenate(outs, axis=0).T.astype(bf16)


def _attention(qT, qiT, wT, k, vT, ki, *, tq, topk):
    B, W, S = qT.shape
    blkT = lambda n: pl.BlockSpec((1, n, tq), lambda b, i: (b, 0, i))
    return pl.pallas_call(
        functools.partial(_attn_kernel, topk=topk),
        out_shape=jax.ShapeDtypeStruct((B, S, W), bf16),
        grid=(B, S // tq),
        in_specs=[blkT(W), blkT(W), blkT(IDX_HEADS),
                  _resident((1, S, W), lambda b, i: (b, 0, 0)),
                  _resident((1, S // tq, W, tq), lambda b, i: (b, 0, 0, 0)),
                  _resident((1, S, LANES), lambda b, i: (b, 0, 0))],
        out_specs=pl.BlockSpec((1, tq, W), lambda b, i: (b, i, 0)),
        scratch_shapes=[pltpu.VMEM((S, tq), f32),
                        pltpu.VMEM((N_HEADS, 2 * HEAD_DIM, tq), bf16),
                        pltpu.VMEM((IDX_HEADS, 2 * IDX_DIM, tq), bf16),
                        pltpu.VMEM((N_HEADS, tq), f32),
                        pltpu.VMEM((N_HEADS, tq), f32),
                        pltpu.VMEM((W, tq), f32)],
        compiler_params=pltpu.CompilerParams(dimension_semantics=("arbitrary", "arbitrary"),
                                             vmem_limit_bytes=VMEM_LIMIT),
        name="dsa_attn",
    )(qT, qiT, wT, k, vT, ki)


def _conv_kernel(zc_ref, zp_ref, w_ref, b_ref, g_ref, beta_ref, o_ref, ext_ref):
    tm = zc_ref.shape[1]
    i = pl.program_id(1)
    prev = jnp.where(i > 0, zp_ref[0], 0.0)
    ext_ref[0:CONV_HALO, :] = prev
    ext_ref[CONV_HALO:CONV_HALO + tm, :] = zc_ref[0]
    acc = jnp.broadcast_to(b_ref[...], (tm, CONV_CH))
    off = CONV_HALO - (CONV_WIDTH - 1)
    for j in range(CONV_WIDTH):
        acc = acc + w_ref[j:j + 1, :] * ext_ref[off + j:off + j + tm, :]
    mu = jnp.mean(acc, axis=-1, keepdims=True)
    d = acc - mu
    var = jnp.mean(d * d, axis=-1, keepdims=True)
    y = d * lax.rsqrt(var + EPS) * g_ref[...] + beta_ref[...]
    o_ref[0] = (y * jax.nn.sigmoid(y)).astype(bf16)


def _conv(z, w_dw, b_dw, ln_g, ln_b, *, tm):
    B, S, C = z.shape
    r = tm // CONV_HALO
    vec = pl.BlockSpec((1, C), lambda b, i: (0, 0))
    return pl.pallas_call(
        _conv_kernel,
        out_shape=jax.ShapeDtypeStruct((B, S, C), bf16),
        grid=(B, S // tm),
        in_specs=[pl.BlockSpec((1, tm, C), lambda b, i: (b, i, 0)),
                  pl.BlockSpec((1, CONV_HALO, C), lambda b, i: (b, jnp.maximum(i * r - 1, 0), 0)),
                  pl.BlockSpec((CONV_WIDTH, C), lambda b, i: (0, 0)),
                  vec, vec, vec],
        out_specs=pl.BlockSpec((1, tm, C), lambda b, i: (b, i, 0)),
        scratch_shapes=[pltpu.VMEM((CONV_HALO + tm, C), f32)],
        compiler_params=pltpu.CompilerParams(dimension_semantics=("arbitrary", "arbitrary")),
        name="conv",
    )(z, z, w_dw.reshape(CONV_WIDTH, C), b_dw.reshape(1, C), ln_g.reshape(1, C), ln_b.reshape(1, C))


def _post_kernel(x_ref, attn_ref, conv_ref, sga_ref, sgb_ref, gm_ref, shf_ref, scf_ref, gf_ref,
                 gffn_ref, gfin_ref, wap_ref, wcp_ref, wout_ref, wfa_ref, wfb_ref, wfo_ref, o_ref,
                 *, final_norm):
    dot = lambda a, w_ref: jnp.dot(a, w_ref[...], preferred_element_type=f32)
    ya = dot(attn_ref[0], wap_ref)
    yc = dot(conv_ref[0], wcp_ref)
    merged = sga_ref[0].astype(f32) * ya + sgb_ref[0].astype(f32) * yc
    x1 = x_ref[0] + gm_ref[0] * dot(merged.astype(bf16), wout_ref)

    ms = jnp.mean(x1 * x1, axis=-1, keepdims=True)
    h = x1 * lax.rsqrt(ms + EPS) * gffn_ref[...]
    hb = (h * (1.0 + scf_ref[0]) + shf_ref[0]).astype(bf16)
    a = dot(hb, wfa_ref)
    b = dot(hb, wfb_ref)
    act = (a * jax.nn.sigmoid(a) * b).astype(bf16)
    x2 = x1 + gf_ref[0] * dot(act, wfo_ref)

    if final_norm:
        ms2 = jnp.mean(x2 * x2, axis=-1, keepdims=True)
        x2 = x2 * lax.rsqrt(ms2 + EPS) * gfin_ref[...]
    o_ref[0] = x2


def _post(x, attn, conv, sga, sgb, g_m, sh_f, sc_f, g_f, norm_ffn_g, norm_final_g,
          w_attn_proj, w_conv_proj, w_out, w_ffn_in, w_ffn_out, *, tm, final_norm):
    B, S, D = x.shape
    d_ff = w_ffn_out.shape[0]
    weights = [w_attn_proj.astype(bf16), w_conv_proj.astype(bf16), w_out.astype(bf16),
               w_ffn_in[:, :d_ff].astype(bf16), w_ffn_in[:, d_ff:].astype(bf16), w_ffn_out.astype(bf16)]
    tile = lambda n: pl.BlockSpec((1, tm, n), lambda b, i: (b, i, 0))
    vec = pl.BlockSpec((1, 1, D), lambda b, i: (b, 0, 0))
    gain = pl.BlockSpec((1, D), lambda b, i: (0, 0))
    in_specs = [tile(D), tile(ATTN_WIDTH), tile(CONV_CH), tile(D), tile(D), vec, vec, vec, vec, gain, gain]
    in_specs += [_resident(w.shape, lambda b, i: (0, 0)) for w in weights]
    return pl.pallas_call(
        functools.partial(_post_kernel, final_norm=final_norm),
        out_shape=jax.ShapeDtypeStruct((B, S, D), f32),
        grid=(B, S // tm),
        in_specs=in_specs,
        out_specs=tile(D),
        compiler_params=pltpu.CompilerParams(dimension_semantics=("arbitrary", "arbitrary"),
                                             vmem_limit_bytes=VMEM_LIMIT),
        name="post",
    )(x, attn, conv, sga, sgb, g_m, sh_f, sc_f, g_f, norm_ffn_g.reshape(1, D),
      norm_final_g.reshape(1, D), *weights)


def kernel(x, c, norm_mix_g, w_in, w_dw, b_dw, conv_ln_g, conv_ln_b, w_attn_proj, w_conv_proj,
           w_out, norm_ffn_g, w_ffn_in, w_ffn_out, w_ada, b_ada, norm_final_g):
    B, S, D = x.shape
    depth = w_in.shape[0]
    topk = min(TOPK_MAX, S // 4)
    tq = 256
    c = c.astype(f32)
    for l in range(depth):
        mod = _ada(c, w_ada[l], b_ada[l])
        sh_m, sc_m, g_m, sh_f, sc_f, g_f = [m.reshape(B, 1, D) for m in jnp.split(mod, N_MOD, axis=-1)]
        qT, qiT, vT, k, ki, wT, z, sga, sgb = _inproj(x, sh_m, sc_m, norm_mix_g[l], w_in[l], tm=512, tk=tq)
        attn = _attention(qT, qiT, wT, k, vT, ki, tq=tq, topk=topk)
        conv = _conv(z, w_dw[l], b_dw[l], conv_ln_g[l], conv_ln_b[l], tm=512)
        x = _post(x, attn, conv, sga, sgb, g_m, sh_f, sc_f, g_f, norm_ffn_g[l], norm_final_g,
                  w_attn_proj[l], w_conv_proj[l], w_out[l], w_ffn_in[l], w_ffn_out[l], tm=256,
                  final_norm=(l == depth - 1))
    return x
```

```python
import functools

import jax
import jax.numpy as jnp
from jax import lax
from jax.experimental import pallas as pl
from jax.experimental.pallas import tpu as pltpu

N_HEADS = 8
HEAD_DIM = 64
ATTN_WIDTH = N_HEADS * HEAD_DIM
IDX_HEADS = 8
IDX_DIM = 64
TOPK_MAX = 256
CONV_CH = 512
CONV_WIDTH = 31
N_MOD = 6
EPS = 1e-6

LANES = 128
CONV_HALO = 32
NEG_BIG = -1e30
VMEM_LIMIT = 56 * 1024 * 1024

f32 = jnp.float32
bf16 = jnp.bfloat16
coarse_t = jnp.bfloat16


def _resident(block_shape, index_map):
    return pl.BlockSpec(block_shape, index_map, pipeline_mode=pl.Buffered(1))


def _ada_kernel(c_ref, w_ref, b_ref, o_ref):
    c = c_ref[...]
    ca = c * jax.nn.sigmoid(c)
    o_ref[...] = jnp.dot(ca, w_ref[...], preferred_element_type=f32,
                         precision=lax.Precision.HIGHEST) + b_ref[...]


def _ada(c, w_ada, b_ada):
    B, D = c.shape
    N = w_ada.shape[1]
    rows = 8
    c_pad = jnp.zeros((rows, D), f32).at[:B].set(c)
    tn = 1024
    out = pl.pallas_call(
        _ada_kernel,
        out_shape=jax.ShapeDtypeStruct((rows, N), f32),
        grid=(N // tn,),
        in_specs=[pl.BlockSpec((rows, D), lambda j: (0, 0)),
                  pl.BlockSpec((D, tn), lambda j: (0, j)),
                  pl.BlockSpec((1, tn), lambda j: (0, j))],
        out_specs=pl.BlockSpec((rows, tn), lambda j: (0, j)),
        compiler_params=pltpu.CompilerParams(dimension_semantics=("arbitrary",)),
        name="ada",
    )(c_pad, w_ada, b_ada.reshape(1, N))
    return out[:B]


def _inproj_kernel(x_ref, sh_ref, sc_ref, g_ref, wt_ref, wk_ref, wki_ref, wwi_ref,
                   wua_ref, wug_ref, wga_ref, wgb_ref,
                   qT_ref, qiT_ref, vT_ref, k_ref, ki_ref, wT_ref, z_ref, sga_ref, sgb_ref,
                   *, tk):
    x = x_ref[0]
    ms = jnp.mean(x * x, axis=-1, keepdims=True)
    h = x * lax.rsqrt(ms + EPS) * g_ref[...]
    h = h * (1.0 + sc_ref[0]) + sh_ref[0]
    hb = h.astype(bf16)

    def proj(w_ref):
        return jnp.dot(hb, w_ref[...], preferred_element_type=f32)

    ptT = proj(wt_ref).T
    qT_ref[0] = ptT[0:ATTN_WIDTH].astype(bf16)
    qiT_ref[0] = ptT[ATTN_WIDTH:2 * ATTN_WIDTH].astype(bf16)
    tm = x.shape[0]
    for t in range(tm // tk):
        vT_ref[0, t] = ptT[2 * ATTN_WIDTH:3 * ATTN_WIDTH, t * tk:(t + 1) * tk].astype(bf16)
    k_ref[0] = proj(wk_ref).astype(bf16)
    ki_ref[0] = proj(wki_ref).astype(bf16)
    wiT = proj(wwi_ref).T
    wT_ref[0] = wiT[0:IDX_HEADS] * (IDX_HEADS ** -0.5)
    z_ref[0] = proj(wua_ref) * jax.nn.sigmoid(proj(wug_ref))
    sga_ref[0] = jax.nn.sigmoid(proj(wga_ref)).astype(bf16)
    sgb_ref[0] = jax.nn.sigmoid(proj(wgb_ref)).astype(bf16)


def _inproj(x, shift, scale, g, w_in, *, tm, tk):
    B, S, D = x.shape
    W = ATTN_WIDTH
    o = 0
    wq = w_in[:, o:o + W]; o += W
    wk = w_in[:, o:o + W]; o += W
    wv = w_in[:, o:o + W]; o += W
    wqi = w_in[:, o:o + IDX_HEADS * IDX_DIM]; o += IDX_HEADS * IDX_DIM
    wki = w_in[:, o:o + IDX_DIM]; o += IDX_DIM
    wwi = w_in[:, o:o + IDX_HEADS]; o += IDX_HEADS
    wua = w_in[:, o:o + CONV_CH]; o += CONV_CH
    wug = w_in[:, o:o + CONV_CH]; o += CONV_CH
    wga = w_in[:, o:o + D]; o += D
    wgb = w_in[:, o:o + D]; o += D
    wt = jnp.concatenate([wq * (HEAD_DIM ** -0.5), wqi * (IDX_DIM ** -0.5), wv], axis=1).astype(bf16)
    pad = lambda w: jnp.pad(w, ((0, 0), (0, LANES - w.shape[1]))).astype(bf16)
    weights = [wt, wk.astype(bf16), pad(wki), pad(wwi), wua.astype(bf16), wug.astype(bf16),
               wga.astype(bf16), wgb.astype(bf16)]

    tile = lambda n: pl.BlockSpec((1, tm, n), lambda b, i: (b, i, 0))
    tileT = lambda n: pl.BlockSpec((1, n, tm), lambda b, i: (b, 0, i))
    vec = pl.BlockSpec((1, 1, D), lambda b, i: (b, 0, 0))
    out_shape = [
        jax.ShapeDtypeStruct((B, W, S), bf16),
        jax.ShapeDtypeStruct((B, W, S), bf16),
        jax.ShapeDtypeStruct((B, S // tk, W, tk), bf16),
        jax.ShapeDtypeStruct((B, S, W), bf16),
        jax.ShapeDtypeStruct((B, S, LANES), bf16),
        jax.ShapeDtypeStruct((B, IDX_HEADS, S), f32),
        jax.ShapeDtypeStruct((B, S, CONV_CH), f32),
        jax.ShapeDtypeStruct((B, S, D), bf16),
        jax.ShapeDtypeStruct((B, S, D), bf16),
    ]
    out_specs = [tileT(W), tileT(W),
                 pl.BlockSpec((1, tm // tk, W, tk), lambda b, i: (b, i, 0, 0)),
                 tile(W), tile(LANES), tileT(IDX_HEADS), tile(CONV_CH), tile(D), tile(D)]
    in_specs = [tile(D), vec, vec, pl.BlockSpec((1, D), lambda b, i: (0, 0))]
    in_specs += [_resident(w.shape, lambda b, i: (0, 0)) for w in weights]
    return pl.pallas_call(
        functools.partial(_inproj_kernel, tk=tk),
        out_shape=out_shape,
        grid=(B, S // tm),
        in_specs=in_specs,
        out_specs=out_specs,
        compiler_params=pltpu.CompilerParams(dimension_semantics=("arbitrary", "arbitrary"),
                                             vmem_limit_bytes=VMEM_LIMIT),
        name="inproj",
    )(x, shift, scale, g.reshape(1, D), *weights)


def _ordered_bits_to_float(u):
    key = u ^ jnp.int32(-2 ** 31)
    bits = key ^ ((key >> 31) & jnp.int32(0x7FFFFFFF))
    return lax.bitcast_convert_type(bits, f32)


def _attn_kernel(qT_ref, qiT_ref, wT_ref, k_ref, vT_ref, ki_ref, o_ref,
                 sc_ref, sb_ref, qpad_ref, qipad_ref, m_ref, l_ref, acc_ref, rowbias_ref, mb_ref, s_ref,
                 *, topk):
    tq = qT_ref.shape[2]
    tk = tq
    i = pl.program_id(1)
    q0 = i * tq
    n_tiles = i + 1

    row = lax.broadcasted_iota(jnp.int32, (tk, tq), 0)
    lane = lax.broadcasted_iota(jnp.int32, (tk, tq), 1)
    lane1 = lax.broadcasted_iota(jnp.int32, (1, tq), 1)

    qT = qT_ref[0]
    qiT = qiT_ref[0]
    first_half = lax.broadcasted_iota(jnp.int32, (2 * HEAD_DIM, tq), 0) < HEAD_DIM
    zeros_half = jnp.zeros((IDX_DIM, tq), bf16)
    for h in range(N_HEADS):
        pair = qT[(h // 2) * 2 * HEAD_DIM:(h // 2 + 1) * 2 * HEAD_DIM]
        keep = first_half if h % 2 == 0 else jnp.logical_not(first_half)
        qpad_ref[h] = jnp.where(keep, pair, jnp.zeros_like(pair))
    for h in range(IDX_HEADS):
        qipad_ref[h] = jnp.concatenate([qiT[h * IDX_DIM:(h + 1) * IDX_DIM], zeros_half], axis=0)

    w_rows = wT_ref[0]

    def score_tile(j, carry):
        k0 = pl.multiple_of(j * tk, tk)
        kit = ki_ref[0, pl.ds(k0, tk), :]
        acc = jnp.zeros((tk, tq), f32)
        for h in range(IDX_HEADS):
            lg = jnp.dot(kit, qipad_ref[h], preferred_element_type=f32)
            acc = acc + w_rows[h:h + 1, :] * jnp.maximum(lg, 0.0)
        causal = (k0 + row) <= (q0 + lane)
        acc = jnp.where(causal, acc, -jnp.inf)
        sc_ref[pl.ds(k0, tk), :] = acc
        sb_ref[pl.ds(k0, tk), :] = acc.astype(coarse_t)
        return carry

    lax.fori_loop(0, n_tiles, score_tile, 0)

    kf = jnp.minimum(topk, q0 + lane1 + 1).astype(f32)

    def count(pred):
        def body(j, acc):
            k0 = pl.multiple_of(j * tk, tk)
            s = sc_ref[pl.ds(k0, tk), :]
            hit = jnp.where(pred(s, k0 + row), 1.0, 0.0)
            return acc + hit.reshape(tk // 8, 8, tq).sum(axis=0)
        acc = lax.fori_loop(0, n_tiles, body, jnp.zeros((8, tq), f32))
        return acc.sum(axis=0, keepdims=True)

    def count_coarse(thr16):
        one = jnp.ones((), coarse_t)
        zero = jnp.zeros((), coarse_t)

        def body(j, acc):
            k0 = pl.multiple_of(j * tk, tk)
            hit = jnp.where(sb_ref[pl.ds(k0, tk), :] >= thr16, one, zero)
            parts = [hit[r * 16:(r + 1) * 16] for r in range(tk // 16)]
            while len(parts) > 1:
                parts = [parts[a] + parts[a + 1] for a in range(0, len(parts), 2)]
            return acc + parts[0].astype(f32)
        acc = lax.fori_loop(0, n_tiles, body, jnp.zeros((16, tq), f32))
        return acc.sum(axis=0, keepdims=True)

    def coarse_value_bits(v):
        return (v ^ jnp.where((v & 0x8000) != 0, 0x8000, 0xFFFF)) << 16

    def bisect_coarse(it, v):
        cand = v | lax.shift_left(jnp.int32(1), 15 - it)
        thr16 = lax.bitcast_convert_type(coarse_value_bits(cand), f32).astype(coarse_t)
        return jnp.where(count_coarse(thr16) >= kf, cand, v)

    v = lax.fori_loop(0, 16, bisect_coarse, jnp.zeros((1, tq), jnp.int32))
    vbits = coarse_value_bits(v)
    u_mid = vbits ^ jnp.where(vbits < 0, jnp.int32(-1), jnp.int32(-2 ** 31))
    u_base = u_mid - 0x8000
    fine_bits = 17

    def fine_cond(c):
        it, _, done = c
        return jnp.logical_and(it < fine_bits, jnp.min(done) < 0.5)

    def fine_body(c):
        it, d, done = c
        cand = d | lax.shift_left(jnp.int32(1), fine_bits - 1 - it)
        thr_c = _ordered_bits_to_float(u_base + cand)
        n = count(lambda s, kpos: s >= thr_c)
        d = jnp.where(jnp.logical_and(done < 0.5, n >= kf), cand, d)
        done = jnp.maximum(done, jnp.where(n == kf, 1.0, 0.0))
        return it + 1, d, done

    _, d, _ = lax.while_loop(fine_cond, fine_body,
                             (jnp.int32(0), jnp.zeros((1, tq), jnp.int32), jnp.zeros((1, tq), f32)))
    u = u_base + d
    thr = _ordered_bits_to_float(u)
    thr_next = _ordered_bits_to_float(u + 1)
    n_ge = count(lambda s, kpos: s >= thr)
    excess = jnp.max(n_ge - kf)

    @pl.when(excess > 0.0)
    def _():
        need = kf - count(lambda s, kpos: s >= thr_next)

        def bisect_index(it, jcut):
            cand = jcut | lax.shift_left(jnp.int32(1), 13 - it)
            c = count(lambda s, kpos: (s >= thr) & (s < thr_next) & (kpos < cand))
            return jnp.where(c <= need, cand, jcut)

        jcut = lax.fori_loop(0, 14, bisect_index, jnp.zeros((1, tq), jnp.int32))

        def drop_tile(j, carry):
            k0 = pl.multiple_of(j * tk, tk)
            s = sc_ref[pl.ds(k0, tk), :]
            drop = (s >= thr) & (s < thr_next) & ((k0 + row) >= jcut)
            sc_ref[pl.ds(k0, tk), :] = jnp.where(drop, -jnp.inf, s)
            return carry

        lax.fori_loop(0, n_tiles, drop_tile, 0)

    m_ref[...] = jnp.full(m_ref.shape, NEG_BIG, f32)
    l_ref[...] = jnp.zeros(l_ref.shape, f32)
    acc_ref[...] = jnp.zeros(acc_ref.shape, f32)

    slopes = [2.0 ** (-8.0 * (h + 1) / N_HEADS) for h in range(N_HEADS)]

    @pl.when(i == 0)
    def _():
        rowf = row.astype(f32)
        for h in range(N_HEADS):
            rowbias_ref[h] = slopes[h] * rowf

    def attn_tile(j, carry):
        k0 = pl.multiple_of(j * tk, tk)
        off = (k0 - q0).astype(f32)
        mb_ref[...] = jnp.where(sc_ref[pl.ds(k0, tk), :] >= thr, 0.0, NEG_BIG)
        m_adj = []
        alphas = []
        for h in range(N_HEADS):
            kp = k_ref[0, pl.ds(k0, tk), (h // 2) * LANES:(h // 2 + 1) * LANES]
            s = jnp.dot(kp, qpad_ref[h], preferred_element_type=f32)
            s = s + rowbias_ref[h] + mb_ref[...]
            s_ref[h] = s
            mt = s.reshape(tk // 8, 8, tq).max(axis=0).max(axis=0, keepdims=True)
            m_old = m_ref[h:h + 1, :]
            m_new = jnp.maximum(m_old, mt + slopes[h] * off)
            m_ref[h:h + 1, :] = m_new
            alphas.append(jnp.exp(m_old - m_new))
            m_adj.append(m_new - slopes[h] * off)
        for h in range(N_HEADS):
            p = jnp.exp(s_ref[h] - m_adj[h])
            psum = p.reshape(tk // 8, 8, tq).sum(axis=0).sum(axis=0, keepdims=True)
            l_ref[h:h + 1, :] = alphas[h] * l_ref[h:h + 1, :] + psum
            vt = vT_ref[0, j, h * HEAD_DIM:(h + 1) * HEAD_DIM, :]
            pv = jnp.dot(vt, p.astype(bf16), preferred_element_type=f32)
            rows = slice(h * HEAD_DIM, (h + 1) * HEAD_DIM)
            acc_ref[rows, :] = alphas[h] * acc_ref[rows, :] + pv
        return carry

    lax.fori_loop(0, n_tiles, attn_tile, 0)

    outs = []
    for h in range(N_HEADS):
        outs.append(acc_ref[h * HEAD_DIM:(h + 1) * HEAD_DIM, :] / l_ref[h:h + 1, :])
    o_ref[0] = jnp.concatenate(outs, axis=0).T.astype(bf16)


def _attention(qT, qiT, wT, k, vT, ki, *, tq, topk):
    B, W, S = qT.shape
    blkT = lambda n: pl.BlockSpec((1, n, tq), lambda b, i: (b, 0, i))
    return pl.pallas_call(
        functools.partial(_attn_kernel, topk=topk),
        out_shape=jax.ShapeDtypeStruct((B, S, W), bf16),
        grid=(B, S // tq),
        in_specs=[blkT(W), blkT(W), blkT(IDX_HEADS),
                  _resident((1, S, W), lambda b, i: (b, 0, 0)),
                  _resident((1, S // tq, W, tq), lambda b, i: (b, 0, 0, 0)),
                  _resident((1, S, LANES), lambda b, i: (b, 0, 0))],
        out_specs=pl.BlockSpec((1, tq, W), lambda b, i: (b, i, 0)),
        scratch_shapes=[pltpu.VMEM((S, tq), f32),
                        pltpu.VMEM((S, tq), coarse_t),
                        pltpu.VMEM((N_HEADS, 2 * HEAD_DIM, tq), bf16),
                        pltpu.VMEM((IDX_HEADS, 2 * IDX_DIM, tq), bf16),
                        pltpu.VMEM((N_HEADS, tq), f32),
                        pltpu.VMEM((N_HEADS, tq), f32),
                        pltpu.VMEM((W, tq), f32),
                        pltpu.VMEM((N_HEADS, tq, tq), f32),
                        pltpu.VMEM((tq, tq), f32),
                        pltpu.VMEM((N_HEADS, tq, tq), f32)],
        compiler_params=pltpu.CompilerParams(dimension_semantics=("arbitrary", "arbitrary"),
                                             vmem_limit_bytes=VMEM_LIMIT),
        name="dsa_attn",
    )(qT, qiT, wT, k, vT, ki)


def _conv_kernel(zc_ref, zp_ref, w_ref, b_ref, g_ref, beta_ref, o_ref, ext_ref):
    tm = zc_ref.shape[1]
    i = pl.program_id(1)
    prev = jnp.where(i > 0, zp_ref[0], 0.0)
    ext_ref[0:CONV_HALO, :] = prev
    ext_ref[CONV_HALO:CONV_HALO + tm, :] = zc_ref[0]
    acc = jnp.broadcast_to(b_ref[...], (tm, CONV_CH))
    off = CONV_HALO - (CONV_WIDTH - 1)
    for j in range(CONV_WIDTH):
        acc = acc + w_ref[j:j + 1, :] * ext_ref[off + j:off + j + tm, :]
    mu = jnp.mean(acc, axis=-1, keepdims=True)
    d = acc - mu
    var = jnp.mean(d * d, axis=-1, keepdims=True)
    y = d * lax.rsqrt(var + EPS) * g_ref[...] + beta_ref[...]
    o_ref[0] = (y * jax.nn.sigmoid(y)).astype(bf16)


def _conv(z, w_dw, b_dw, ln_g, ln_b, *, tm):
    B, S, C = z.shape
    r = tm // CONV_HALO
    vec = pl.BlockSpec((1, C), lambda b, i: (0, 0))
    return pl.pallas_call(
        _conv_kernel,
        out_shape=jax.ShapeDtypeStruct((B, S, C), bf16),
        grid=(B, S // tm),
        in_specs=[pl.BlockSpec((1, tm, C), lambda b, i: (b, i, 0)),
                  pl.BlockSpec((1, CONV_HALO, C), lambda b, i: (b, jnp.maximum(i * r - 1, 0), 0)),
                  pl.BlockSpec((CONV_WIDTH, C), lambda b, i: (0, 0)),
                  vec, vec, vec],
        out_specs=pl.BlockSpec((1, tm, C), lambda b, i: (b, i, 0)),
        scratch_shapes=[pltpu.VMEM((CONV_HALO + tm, C), f32)],
        compiler_params=pltpu.CompilerParams(dimension_semantics=("arbitrary", "arbitrary")),
        name="conv",
    )(z, z, w_dw.reshape(CONV_WIDTH, C), b_dw.reshape(1, C), ln_g.reshape(1, C), ln_b.reshape(1, C))


def _post_kernel(x_ref, attn_ref, conv_ref, sga_ref, sgb_ref, gm_ref, shf_ref, scf_ref, gf_ref,
                 gffn_ref, gfin_ref, wap_ref, wcp_ref, wout_ref, wfa_ref, wfb_ref, wfo_ref, o_ref,
                 *, final_norm):
    dot = lambda a, w_ref: jnp.dot(a, w_ref[...], preferred_element_type=f32)
    ya = dot(attn_ref[0], wap_ref)
    yc = dot(conv_ref[0], wcp_ref)
    merged = sga_ref[0].astype(f32) * ya + sgb_ref[0].astype(f32) * yc
    x1 = x_ref[0] + gm_ref[0] * dot(merged.astype(bf16), wout_ref)

    ms = jnp.mean(x1 * x1, axis=-1, keepdims=True)
    h = x1 * lax.rsqrt(ms + EPS) * gffn_ref[...]
    hb = (h * (1.0 + scf_ref[0]) + shf_ref[0]).astype(bf16)
    a = dot(hb, wfa_ref)
    b = dot(hb, wfb_ref)
    act = (a * jax.nn.sigmoid(a) * b).astype(bf16)
    x2 = x1 + gf_ref[0] * dot(act, wfo_ref)

    if final_norm:
        ms2 = jnp.mean(x2 * x2, axis=-1, keepdims=True)
        x2 = x2 * lax.rsqrt(ms2 + EPS) * gfin_ref[...]
    o_ref[0] = x2


def _post(x, attn, conv, sga, sgb, g_m, sh_f, sc_f, g_f, norm_ffn_g, norm_final_g,
          w_attn_proj, w_conv_proj, w_out, w_ffn_in, w_ffn_out, *, tm, final_norm):
    B, S, D = x.shape
    d_ff = w_ffn_out.shape[0]
    weights = [w_attn_proj.astype(bf16), w_conv_proj.astype(bf16), w_out.astype(bf16),
               w_ffn_in[:, :d_ff].astype(bf16), w_ffn_in[:, d_ff:].astype(bf16), w_ffn_out.astype(bf16)]
    tile = lambda n: pl.BlockSpec((1, tm, n), lambda b, i: (b, i, 0))
    vec = pl.BlockSpec((1, 1, D), lambda b, i: (b, 0, 0))
    gain = pl.BlockSpec((1, D), lambda b, i: (0, 0))
    in_specs = [tile(D), tile(ATTN_WIDTH), tile(CONV_CH), tile(D), tile(D), vec, vec, vec, vec, gain, gain]
    in_specs += [_resident(w.shape, lambda b, i: (0, 0)) for w in weights]
    return pl.pallas_call(
        functools.partial(_post_kernel, final_norm=final_norm),
        out_shape=jax.ShapeDtypeStruct((B, S, D), f32),
        grid=(B, S // tm),
        in_specs=in_specs,
        out_specs=tile(D),
        compiler_params=pltpu.CompilerParams(dimension_semantics=("arbitrary", "arbitrary"),
                                             vmem_limit_bytes=VMEM_LIMIT),
        name="post",
    )(x, attn, conv, sga, sgb, g_m, sh_f, sc_f, g_f, norm_ffn_g.reshape(1, D),
      norm_final_g.reshape(1, D), *weights)


def kernel(x, c, norm_mix_g, w_in, w_dw, b_dw, conv_ln_g, conv_ln_b, w_attn_proj, w_conv_proj,
           w_out, norm_ffn_g, w_ffn_in, w_ffn_out, w_ada, b_ada, norm_final_g):
    B, S, D = x.shape
    depth = w_in.shape[0]
    topk = min(TOPK_MAX, S // 4)
    tq = 256
    c = c.astype(f32)
    for l in range(depth):
        mod = _ada(c, w_ada[l], b_ada[l])
        sh_m, sc_m, g_m, sh_f, sc_f, g_f = [m.reshape(B, 1, D) for m in jnp.split(mod, N_MOD, axis=-1)]
        qT, qiT, vT, k, ki, wT, z, sga, sgb = _inproj(x, sh_m, sc_m, norm_mix_g[l], w_in[l], tm=512, tk=tq)
        attn = _attention(qT, qiT, wT, k, vT, ki, tq=tq, topk=topk)
        conv = _conv(z, w_dw[l], b_dw[l], conv_ln_g[l], conv_ln_b[l], tm=512)
        x = _post(x, attn, conv, sga, sgb, g_m, sh_f, sc_f, g_f, norm_ffn_g[l], norm_final_g,
                  w_attn_proj[l], w_conv_proj[l], w_out[l], w_ffn_in[l], w_ffn_out[l], tm=256,
                  final_norm=(l == depth - 1))
    return x
```

```python
import functools
import math

import jax
import jax.numpy as jnp
from jax import lax
from jax.experimental import pallas as pl
from jax.experimental.pallas import tpu as pltpu

N_HEADS = 8
HEAD_DIM = 64
ATTN_WIDTH = N_HEADS * HEAD_DIM
IDX_HEADS = 8
IDX_DIM = 64
TOPK_MAX = 256
CONV_CH = 512
CONV_WIDTH = 31
N_MOD = 6
EPS = 1e-6

LANES = 128
SUBLANES = 8
PACKED_ROWS = 16
CONV_HALO = 32
NEG_BIG = -1e30
VMEM_LIMIT = 56 * 1024 * 1024

ATT_TQ = 256
ATT_CK = 512
ATT_CC = 1024
SUM_ROWS = 16
LOG2E = math.log2(math.e)

f32 = jnp.float32
bf16 = jnp.bfloat16
coarse_t = jnp.bfloat16


def _resident(block_shape, index_map):
    return pl.BlockSpec(block_shape, index_map, pipeline_mode=pl.Buffered(1))


REDUCE_CHAINS = 2


def _reduce_rows(op, x, rows):
    groups = [x[r * rows:(r + 1) * rows] for r in range(x.shape[0] // rows)]
    chains = groups[:REDUCE_CHAINS]
    for r, g in enumerate(groups[REDUCE_CHAINS:]):
        chains[r % REDUCE_CHAINS] = op(chains[r % REDUCE_CHAINS], g)
    while len(chains) > 1:
        chains = [op(chains[a], chains[a + 1]) for a in range(0, len(chains), 2)]
    return chains[0]


def _ada_kernel(c_ref, w_ref, b_ref, o_ref):
    c = c_ref[...]
    ca = c * jax.nn.sigmoid(c)
    o_ref[...] = jnp.dot(ca, w_ref[...], preferred_element_type=f32,
                         precision=lax.Precision.HIGHEST) + b_ref[...]


def _ada(c, w_ada, b_ada):
    B, D = c.shape
    N = w_ada.shape[1]
    rows = SUBLANES
    c_pad = jnp.zeros((rows, D), f32).at[:B].set(c)
    tn = 1024
    out = pl.pallas_call(
        _ada_kernel,
        out_shape=jax.ShapeDtypeStruct((rows, N), f32),
        grid=(N // tn,),
        in_specs=[pl.BlockSpec((rows, D), lambda j: (0, 0)),
                  pl.BlockSpec((D, tn), lambda j: (0, j)),
                  pl.BlockSpec((1, tn), lambda j: (0, j))],
        out_specs=pl.BlockSpec((rows, tn), lambda j: (0, j)),
        compiler_params=pltpu.CompilerParams(dimension_semantics=("arbitrary",)),
        name="ada",
    )(c_pad, w_ada, b_ada.reshape(1, N))
    return out[:B]


def _inproj_kernel(x_ref, sh_ref, sc_ref, g_ref, wt_ref, wk_ref, wki_ref, wwi_ref,
                   wua_ref, wug_ref, wga_ref, wgb_ref,
                   qT_ref, qiT_ref, vT_ref, k_ref, ki_ref, wT_ref, z_ref, sga_ref, sgb_ref,
                   *, ck):
    x = x_ref[0]
    ms = jnp.mean(x * x, axis=-1, keepdims=True)
    h = x * lax.rsqrt(ms + EPS) * g_ref[...]
    h = h * (1.0 + sc_ref[0]) + sh_ref[0]
    hb = h.astype(bf16)

    def proj(w_ref):
        return jnp.dot(hb, w_ref[...], preferred_element_type=f32)

    ptT = proj(wt_ref).T
    qT_ref[0] = ptT[0:ATTN_WIDTH].astype(bf16)
    qiT_ref[0] = ptT[ATTN_WIDTH:2 * ATTN_WIDTH].astype(bf16)
    tm = x.shape[0]
    for t in range(tm // ck):
        vT_ref[0, t] = ptT[2 * ATTN_WIDTH:3 * ATTN_WIDTH, t * ck:(t + 1) * ck].astype(bf16)
    k_ref[0] = proj(wk_ref).astype(bf16)
    ki_ref[0] = proj(wki_ref).astype(bf16)
    wiT = proj(wwi_ref).T
    wT_ref[0] = wiT[0:IDX_HEADS] * (IDX_HEADS ** -0.5)
    z_ref[0] = proj(wua_ref) * jax.nn.sigmoid(proj(wug_ref))
    sga_ref[0] = jax.nn.sigmoid(proj(wga_ref)).astype(bf16)
    sgb_ref[0] = jax.nn.sigmoid(proj(wgb_ref)).astype(bf16)


def _inproj(x, shift, scale, g, w_in, *, tm, ck):
    B, S, D = x.shape
    W = ATTN_WIDTH
    o = 0
    wq = w_in[:, o:o + W]; o += W
    wk = w_in[:, o:o + W]; o += W
    wv = w_in[:, o:o + W]; o += W
    wqi = w_in[:, o:o + IDX_HEADS * IDX_DIM]; o += IDX_HEADS * IDX_DIM
    wki = w_in[:, o:o + IDX_DIM]; o += IDX_DIM
    wwi = w_in[:, o:o + IDX_HEADS]; o += IDX_HEADS
    wua = w_in[:, o:o + CONV_CH]; o += CONV_CH
    wug = w_in[:, o:o + CONV_CH]; o += CONV_CH
    wga = w_in[:, o:o + D]; o += D
    wgb = w_in[:, o:o + D]; o += D
    wt = jnp.concatenate([wq * (HEAD_DIM ** -0.5 * LOG2E), wqi * (IDX_DIM ** -0.5), wv], axis=1).astype(bf16)
    pad = lambda w: jnp.pad(w, ((0, 0), (0, LANES - w.shape[1]))).astype(bf16)
    weights = [wt, wk.astype(bf16), pad(wki), pad(wwi), wua.astype(bf16), wug.astype(bf16),
               wga.astype(bf16), wgb.astype(bf16)]

    tile = lambda n: pl.BlockSpec((1, tm, n), lambda b, i: (b, i, 0))
    tileT = lambda n: pl.BlockSpec((1, n, tm), lambda b, i: (b, 0, i))
    vec = pl.BlockSpec((1, 1, D), lambda b, i: (b, 0, 0))
    out_shape = [
        jax.ShapeDtypeStruct((B, W, S), bf16),
        jax.ShapeDtypeStruct((B, W, S), bf16),
        jax.ShapeDtypeStruct((B, S // ck, W, ck), bf16),
        jax.ShapeDtypeStruct((B, S, W), bf16),
        jax.ShapeDtypeStruct((B, S, LANES), bf16),
        jax.ShapeDtypeStruct((B, IDX_HEADS, S), f32),
        jax.ShapeDtypeStruct((B, S, CONV_CH), f32),
        jax.ShapeDtypeStruct((B, S, D), bf16),
        jax.ShapeDtypeStruct((B, S, D), bf16),
    ]
    out_specs = [tileT(W), tileT(W),
                 pl.BlockSpec((1, tm // ck, W, ck), lambda b, i: (b, i, 0, 0)),
                 tile(W), tile(LANES), tileT(IDX_HEADS), tile(CONV_CH), tile(D), tile(D)]
    in_specs = [tile(D), vec, vec, pl.BlockSpec((1, D), lambda b, i: (0, 0))]
    in_specs += [_resident(w.shape, lambda b, i: (0, 0)) for w in weights]
    return pl.pallas_call(
        functools.partial(_inproj_kernel, ck=ck),
        out_shape=out_shape,
        grid=(B, S // tm),
        in_specs=in_specs,
        out_specs=out_specs,
        compiler_params=pltpu.CompilerParams(dimension_semantics=("arbitrary", "arbitrary"),
                                             vmem_limit_bytes=VMEM_LIMIT),
        name="inproj",
    )(x, shift, scale, g.reshape(1, D), *weights)


def _ordered_bits_to_float(u):
    key = u ^ jnp.int32(-2 ** 31)
    bits = key ^ ((key >> 31) & jnp.int32(0x7FFFFFFF))
    return lax.bitcast_convert_type(bits, f32)


def _attn_kernel(qT_ref, qiT_ref, wT_ref, k_ref, vT_ref, ki_ref, o_ref,
                 sc_ref, sb_ref, qpad_ref, qipad_ref, m_ref, acc_ref, rowbias_ref, mb_ref, s_ref,
                 *, topk):
    tq = qT_ref.shape[2]
    ck = vT_ref.shape[3]
    cc = ATT_CC
    i = pl.program_id(1)
    q0 = i * tq
    n_keys = q0 + tq
    n_ck = (n_keys + ck - 1) // ck
    n_cc = (n_keys + cc - 1) // cc

    lane1 = lax.broadcasted_iota(jnp.int32, (1, tq), 1)

    qT = qT_ref[0]
    qiT = qiT_ref[0]
    first_half = lax.broadcasted_iota(jnp.int32, (2 * HEAD_DIM, tq), 0) < HEAD_DIM
    zeros_half = jnp.zeros((IDX_DIM, tq), bf16)
    for h in range(N_HEADS):
        pair = qT[(h // 2) * 2 * HEAD_DIM:(h // 2 + 1) * 2 * HEAD_DIM]
        keep = first_half if h % 2 == 0 else jnp.logical_not(first_half)
        qpad_ref[h] = jnp.where(keep, pair, jnp.zeros_like(pair))
    for h in range(IDX_HEADS):
        qipad_ref[h] = jnp.concatenate([qiT[h * IDX_DIM:(h + 1) * IDX_DIM], zeros_half], axis=0)

    w_rows = wT_ref[0]
    row_t = lax.broadcasted_iota(jnp.int32, (tq, tq), 0)
    lane_t = lax.broadcasted_iota(jnp.int32, (tq, tq), 1)

    def score_chunk(j, carry):
        for t in range(ck // tq):
            k0 = pl.multiple_of(j * ck + t * tq, tq)
            kit = ki_ref[0, pl.ds(k0, tq), :]
            acc = jnp.zeros((tq, tq), f32)
            for h in range(IDX_HEADS):
                lg = jnp.dot(kit, qipad_ref[h], preferred_element_type=f32)
                acc = acc + w_rows[h:h + 1, :] * jnp.maximum(lg, 0.0)
            causal = (k0 + row_t) <= (q0 + lane_t)
            acc = jnp.where(causal, acc, -jnp.inf)
            sc_ref[pl.ds(k0, tq), :] = acc
            sb_ref[pl.ds(k0, tq), :] = acc.astype(coarse_t)
        return carry

    lax.fori_loop(0, n_ck, score_chunk, 0)

    @pl.when(n_ck * ck < n_cc * cc)
    def _():
        for r0 in range(ck, cc, ck):
            start = pl.multiple_of(n_ck * ck + (r0 - ck), ck)
            sc_ref[pl.ds(start, ck), :] = jnp.full((ck, tq), -jnp.inf, f32)
            sb_ref[pl.ds(start, ck), :] = jnp.full((ck, tq), -jnp.inf, coarse_t)

    kf = jnp.minimum(topk, q0 + lane1 + 1).astype(f32)
    row_c = lax.broadcasted_iota(jnp.int32, (cc, tq), 0)

    def count(pred):
        def body(j, acc):
            k0 = pl.multiple_of(j * cc, cc)
            s = sc_ref[pl.ds(k0, cc), :]
            hit = jnp.where(pred(s, k0 + row_c), 1.0, 0.0)
            return acc + _reduce_rows(jnp.add, hit, SUBLANES)
        acc = lax.fori_loop(0, n_cc, body, jnp.zeros((SUBLANES, tq), f32))
        return acc.sum(axis=0, keepdims=True)

    def count_coarse(thr16):
        one = jnp.ones((), coarse_t)
        zero = jnp.zeros((), coarse_t)

        def body(j, acc):
            k0 = pl.multiple_of(j * cc, cc)
            hit = jnp.where(sb_ref[pl.ds(k0, cc), :] >= thr16, one, zero)
            return acc + _reduce_rows(jnp.add, hit, PACKED_ROWS).astype(f32)
        acc = lax.fori_loop(0, n_cc, body, jnp.zeros((PACKED_ROWS, tq), f32))
        return acc.sum(axis=0, keepdims=True)

    def coarse_value_bits(v):
        return (v ^ jnp.where((v & 0x8000) != 0, 0x8000, 0xFFFF)) << 16

    def bisect_coarse(it, v):
        cand = v | lax.shift_left(jnp.int32(1), 15 - it)
        thr16 = lax.bitcast_convert_type(coarse_value_bits(cand), f32).astype(coarse_t)
        return jnp.where(count_coarse(thr16) >= kf, cand, v)

    v = lax.fori_loop(0, 16, bisect_coarse, jnp.zeros((1, tq), jnp.int32))
    vbits = coarse_value_bits(v)
    u_mid = vbits ^ jnp.where(vbits < 0, jnp.int32(-1), jnp.int32(-2 ** 31))
    u_base = u_mid - 0x8000
    fine_bits = 17

    def fine_cond(c):
        it, _, done = c
        return jnp.logical_and(it < fine_bits, jnp.min(done) < 0.5)

    def fine_body(c):
        it, d, done = c
        cand = d | lax.shift_left(jnp.int32(1), fine_bits - 1 - it)
        thr_c = _ordered_bits_to_float(u_base + cand)
        n = count(lambda s, kpos: s >= thr_c)
        d = jnp.where(jnp.logical_and(done < 0.5, n >= kf), cand, d)
        done = jnp.maximum(done, jnp.where(n == kf, 1.0, 0.0))
        return it + 1, d, done

    _, d, _ = lax.while_loop(fine_cond, fine_body,
                             (jnp.int32(0), jnp.zeros((1, tq), jnp.int32), jnp.zeros((1, tq), f32)))
    u = u_base + d
    thr = _ordered_bits_to_float(u)
    thr_next = _ordered_bits_to_float(u + 1)
    n_ge = count(lambda s, kpos: s >= thr)
    excess = jnp.max(n_ge - kf)

    @pl.when(excess > 0.0)
    def _():
        need = kf - count(lambda s, kpos: s >= thr_next)

        def bisect_index(it, jcut):
            cand = jcut | lax.shift_left(jnp.int32(1), 13 - it)
            c = count(lambda s, kpos: (s >= thr) & (s < thr_next) & (kpos < cand))
            return jnp.where(c <= need, cand, jcut)

        jcut = lax.fori_loop(0, 14, bisect_index, jnp.zeros((1, tq), jnp.int32))

        def drop_chunk(j, carry):
            k0 = pl.multiple_of(j * cc, cc)
            s = sc_ref[pl.ds(k0, cc), :]
            drop = (s >= thr) & (s < thr_next) & ((k0 + row_c) >= jcut)
            sc_ref[pl.ds(k0, cc), :] = jnp.where(drop, -jnp.inf, s)
            return carry

        lax.fori_loop(0, n_cc, drop_chunk, 0)

    m_ref[...] = jnp.full(m_ref.shape, NEG_BIG, f32)
    acc_ref[...] = jnp.zeros(acc_ref.shape, f32)

    slopes = [LOG2E * 2.0 ** (-8.0 * (h + 1) / N_HEADS) for h in range(N_HEADS)]

    @pl.when(i == 0)
    def _():
        rowf = lax.broadcasted_iota(jnp.int32, (ck, tq), 0).astype(f32)
        for h in range(N_HEADS):
            rowbias_ref[h] = slopes[h] * rowf

    ones_rows = jnp.ones((SUM_ROWS, ck), bf16)

    def attn_chunk(j, carry):
        k0 = pl.multiple_of(j * ck, ck)
        off = (k0 - q0).astype(f32)
        mb_ref[...] = jnp.where(sc_ref[pl.ds(k0, ck), :] >= thr, 0.0, NEG_BIG)
        m_adj = []
        alphas = []
        for h in range(N_HEADS):
            kp = k_ref[0, pl.ds(k0, ck), (h // 2) * LANES:(h // 2 + 1) * LANES]
            s = jnp.dot(kp, qpad_ref[h], preferred_element_type=f32)
            s = s + rowbias_ref[h] + mb_ref[...]
            s_ref[h] = s
            mt = _reduce_rows(jnp.maximum, s, SUBLANES).max(axis=0, keepdims=True)
            m_old = m_ref[h:h + 1, :]
            m_new = jnp.maximum(m_old, mt + slopes[h] * off)
            m_ref[h:h + 1, :] = m_new
            alphas.append(jnp.exp2(m_old - m_new))
            m_adj.append(m_new - slopes[h] * off)
        for h in range(N_HEADS):
            p = jnp.exp2(s_ref[h] - m_adj[h]).astype(bf16)
            vt = jnp.concatenate([vT_ref[0, j, h * HEAD_DIM:(h + 1) * HEAD_DIM, :], ones_rows], axis=0)
            pv = jnp.dot(vt, p, preferred_element_type=f32)
            acc_ref[h] = alphas[h] * acc_ref[h] + pv
        return carry

    lax.fori_loop(0, n_ck, attn_chunk, 0)

    outs = []
    for h in range(N_HEADS):
        outs.append(acc_ref[h, 0:HEAD_DIM, :] / acc_ref[h, HEAD_DIM:HEAD_DIM + 1, :])
    o_ref[0] = jnp.concatenate(outs, axis=0).T.astype(bf16)


def _attention(qT, qiT, wT, k, vT, ki, *, topk):
    B, W, S = qT.shape
    tq, ck, cc = ATT_TQ, ATT_CK, ATT_CC
    assert S % cc == 0 and cc == 2 * ck and ck % tq == 0 and vT.shape[3] == ck
    blkT = lambda n: pl.BlockSpec((1, n, tq), lambda b, i: (b, 0, i))
    return pl.pallas_call(
        functools.partial(_attn_kernel, topk=topk),
        out_shape=jax.ShapeDtypeStruct((B, S, W), bf16),
        grid=(B, S // tq),
        in_specs=[blkT(W), blkT(W), blkT(IDX_HEADS),
                  _resident((1, S, W), lambda b, i: (b, 0, 0)),
                  _resident((1, S // ck, W, ck), lambda b, i: (b, 0, 0, 0)),
                  _resident((1, S, LANES), lambda b, i: (b, 0, 0))],
        out_specs=pl.BlockSpec((1, tq, W), lambda b, i: (b, i, 0)),
        scratch_shapes=[pltpu.VMEM((S, tq), f32),
                        pltpu.VMEM((S, tq), coarse_t),
                        pltpu.VMEM((N_HEADS, 2 * HEAD_DIM, tq), bf16),
                        pltpu.VMEM((IDX_HEADS, 2 * IDX_DIM, tq), bf16),
                        pltpu.VMEM((N_HEADS, tq), f32),
                        pltpu.VMEM((N_HEADS, HEAD_DIM + SUM_ROWS, tq), f32),
                        pltpu.VMEM((N_HEADS, ck, tq), f32),
                        pltpu.VMEM((ck, tq), f32),
                        pltpu.VMEM((N_HEADS, ck, tq), f32)],
        compiler_params=pltpu.CompilerParams(dimension_semantics=("arbitrary", "arbitrary"),
                                             vmem_limit_bytes=VMEM_LIMIT),
        name="dsa_attn",
    )(qT, qiT, wT, k, vT, ki)


def _conv_kernel(zc_ref, zp_ref, w_ref, b_ref, g_ref, beta_ref, o_ref, ext_ref):
    tm = zc_ref.shape[1]
    i = pl.program_id(1)
    prev = jnp.where(i > 0, zp_ref[0], 0.0)
    ext_ref[0:CONV_HALO, :] = prev
    ext_ref[CONV_HALO:CONV_HALO + tm, :] = zc_ref[0]
    n_ext = CONV_HALO + tm
    ext = ext_ref[...]
    acc = jnp.broadcast_to(b_ref[...], (tm, CONV_CH))
    first = CONV_HALO - (CONV_WIDTH - 1)
    for b in range(SUBLANES):
        offs = [o for o in range(first, first + CONV_WIDTH) if o % SUBLANES == b]
        if not offs:
            continue
        shifted = ext if b == 0 else pltpu.roll(ext, n_ext - b, axis=0)
        for o in offs:
            a = o - b
            acc = acc + w_ref[o - first:o - first + 1, :] * shifted[a:a + tm]
    mu = jnp.mean(acc, axis=-1, keepdims=True)
    d = acc - mu
    var = jnp.mean(d * d, axis=-1, keepdims=True)
    y = d * lax.rsqrt(var + EPS) * g_ref[...] + beta_ref[...]
    o_ref[0] = (y * jax.nn.sigmoid(y)).astype(bf16)


def _conv(z, w_dw, b_dw, ln_g, ln_b, *, tm):
    B, S, C = z.shape
    r = tm // CONV_HALO
    vec = pl.BlockSpec((1, C), lambda b, i: (0, 0))
    return pl.pallas_call(
        _conv_kernel,
        out_shape=jax.ShapeDtypeStruct((B, S, C), bf16),
        grid=(B, S // tm),
        in_specs=[pl.BlockSpec((1, tm, C), lambda b, i: (b, i, 0)),
                  pl.BlockSpec((1, CONV_HALO, C), lambda b, i: (b, jnp.maximum(i * r - 1, 0), 0)),
                  pl.BlockSpec((CONV_WIDTH, C), lambda b, i: (0, 0)),
                  vec, vec, vec],
        out_specs=pl.BlockSpec((1, tm, C), lambda b, i: (b, i, 0)),
        scratch_shapes=[pltpu.VMEM((CONV_HALO + tm, C), f32)],
        compiler_params=pltpu.CompilerParams(dimension_semantics=("arbitrary", "arbitrary")),
        name="conv",
    )(z, z, w_dw.reshape(CONV_WIDTH, C), b_dw.reshape(1, C), ln_g.reshape(1, C), ln_b.reshape(1, C))


def _post_kernel(x_ref, attn_ref, conv_ref, sga_ref, sgb_ref, gm_ref, shf_ref, scf_ref, gf_ref,
                 gffn_ref, gfin_ref, wap_ref, wcp_ref, wout_ref, wfa_ref, wfb_ref, wfo_ref, o_ref,
                 *, final_norm):
    dot = lambda a, w_ref: jnp.dot(a, w_ref[...], preferred_element_type=f32)
    ya = dot(attn_ref[0], wap_ref)
    yc = dot(conv_ref[0], wcp_ref)
    merged = sga_ref[0].astype(f32) * ya + sgb_ref[0].astype(f32) * yc
    x1 = x_ref[0] + gm_ref[0] * dot(merged.astype(bf16), wout_ref)

    ms = jnp.mean(x1 * x1, axis=-1, keepdims=True)
    h = x1 * lax.rsqrt(ms + EPS) * gffn_ref[...]
    hb = (h * (1.0 + scf_ref[0]) + shf_ref[0]).astype(bf16)
    a = dot(hb, wfa_ref)
    b = dot(hb, wfb_ref)
    act = (a * jax.nn.sigmoid(a) * b).astype(bf16)
    x2 = x1 + gf_ref[0] * dot(act, wfo_ref)

    if final_norm:
        ms2 = jnp.mean(x2 * x2, axis=-1, keepdims=True)
        x2 = x2 * lax.rsqrt(ms2 + EPS) * gfin_ref[...]
    o_ref[0] = x2


def _post(x, attn, conv, sga, sgb, g_m, sh_f, sc_f, g_f, norm_ffn_g, norm_final_g,
          w_attn_proj, w_conv_proj, w_out, w_ffn_in, w_ffn_out, *, tm, final_norm):
    B, S, D = x.shape
    d_ff = w_ffn_out.shape[0]
    weights = [w_attn_proj.astype(bf16), w_conv_proj.astype(bf16), w_out.astype(bf16),
               w_ffn_in[:, :d_ff].astype(bf16), w_ffn_in[:, d_ff:].astype(bf16), w_ffn_out.astype(bf16)]
    tile = lambda n: pl.BlockSpec((1, tm, n), lambda b, i: (b, i, 0))
    vec = pl.BlockSpec((1, 1, D), lambda b, i: (b, 0, 0))
    gain = pl.BlockSpec((1, D), lambda b, i: (0, 0))
    in_specs = [tile(D), tile(ATTN_WIDTH), tile(CONV_CH), tile(D), tile(D), vec, vec, vec, vec, gain, gain]
    in_specs += [_resident(w.shape, lambda b, i: (0, 0)) for w in weights]
    return pl.pallas_call(
        functools.partial(_post_kernel, final_norm=final_norm),
        out_shape=jax.ShapeDtypeStruct((B, S, D), f32),
        grid=(B, S // tm),
        in_specs=in_specs,
        out_specs=tile(D),
        compiler_params=pltpu.CompilerParams(dimension_semantics=("arbitrary", "arbitrary"),
                                             vmem_limit_bytes=VMEM_LIMIT),
        name="post",
    )(x, attn, conv, sga, sgb, g_m, sh_f, sc_f, g_f, norm_ffn_g.reshape(1, D),
      norm_final_g.reshape(1, D), *weights)


def kernel(x, c, norm_mix_g, w_in, w_dw, b_dw, conv_ln_g, conv_ln_b, w_attn_proj, w_conv_proj,
           w_out, norm_ffn_g, w_ffn_in, w_ffn_out, w_ada, b_ada, norm_final_g):
    B, S, D = x.shape
    depth = w_in.shape[0]
    topk = min(TOPK_MAX, S // 4)
    c = c.astype(f32)
    for l in range(depth):
        mod = _ada(c, w_ada[l], b_ada[l])
        sh_m, sc_m, g_m, sh_f, sc_f, g_f = [m.reshape(B, 1, D) for m in jnp.split(mod, N_MOD, axis=-1)]
        qT, qiT, vT, k, ki, wT, z, sga, sgb = _inproj(x, sh_m, sc_m, norm_mix_g[l], w_in[l],
                                                      tm=512, ck=ATT_CK)
        attn = _attention(qT, qiT, wT, k, vT, ki, topk=topk)
        conv = _conv(z, w_dw[l], b_dw[l], conv_ln_g[l], conv_ln_b[l], tm=512)
        x = _post(x, attn, conv, sga, sgb, g_m, sh_f, sc_f, g_f, norm_ffn_g[l], norm_final_g,
                  w_attn_proj[l], w_conv_proj[l], w_out[l], w_ffn_in[l], w_ffn_out[l], tm=256,
                  final_norm=(l == depth - 1))
    return x
```

```python
import functools
import math

import jax
import jax.numpy as jnp
import numpy as np
from jax import lax
from jax.experimental import pallas as pl
from jax.experimental.pallas import tpu as pltpu

N_HEADS = 8
HEAD_DIM = 64
ATTN_WIDTH = N_HEADS * HEAD_DIM
IDX_HEADS = 8
IDX_DIM = 64
TOPK_MAX = 256
CONV_CH = 512
CONV_WIDTH = 31
N_MOD = 6
EPS = 1e-6

LANES = 128
SUBLANES = 8
PACKED_ROWS = 16
CONV_HALO = 32
NEG_BIG = -1e30
VMEM_LIMIT = 56 * 1024 * 1024

ATT_TQ = 256
ATT_CK = 512
ATT_CC = 1024
SUM_ROWS = 16
FINE_CHECKS = (12, 14)
LOG2E = math.log2(math.e)

f32 = jnp.float32
bf16 = jnp.bfloat16
coarse_t = jnp.bfloat16


def _resident(block_shape, index_map):
    return pl.BlockSpec(block_shape, index_map, pipeline_mode=pl.Buffered(1))


REDUCE_CHAINS = 2


def _reduce_rows(op, x, rows):
    groups = [x[r * rows:(r + 1) * rows] for r in range(x.shape[0] // rows)]
    chains = groups[:REDUCE_CHAINS]
    for r, g in enumerate(groups[REDUCE_CHAINS:]):
        chains[r % REDUCE_CHAINS] = op(chains[r % REDUCE_CHAINS], g)
    while len(chains) > 1:
        chains = [op(chains[a], chains[a + 1]) for a in range(0, len(chains), 2)]
    return chains[0]


def _ada_kernel(c_ref, w_ref, b_ref, o_ref):
    c = c_ref[...]
    ca = c * jax.nn.sigmoid(c)
    o_ref[...] = jnp.dot(ca, w_ref[...], preferred_element_type=f32,
                         precision=lax.Precision.HIGHEST) + b_ref[...]


def _ada(c, w_ada, b_ada):
    B, D = c.shape
    N = w_ada.shape[1]
    rows = SUBLANES
    c_pad = jnp.zeros((rows, D), f32).at[:B].set(c)
    tn = 1024
    out = pl.pallas_call(
        _ada_kernel,
        out_shape=jax.ShapeDtypeStruct((rows, N), f32),
        grid=(N // tn,),
        in_specs=[pl.BlockSpec((rows, D), lambda j: (0, 0)),
                  pl.BlockSpec((D, tn), lambda j: (0, j)),
                  pl.BlockSpec((1, tn), lambda j: (0, j))],
        out_specs=pl.BlockSpec((rows, tn), lambda j: (0, j)),
        compiler_params=pltpu.CompilerParams(dimension_semantics=("arbitrary",)),
        name="ada",
    )(c_pad, w_ada, b_ada.reshape(1, N))
    return out[:B]


def _inproj_kernel(x_ref, sh_ref, sc_ref, g_ref, wt_ref, wk_ref, wki_ref, wwi_ref,
                   wua_ref, wug_ref, wga_ref, wgb_ref,
                   qT_ref, qiT_ref, vT_ref, k_ref, ki_ref, wT_ref, z_ref, sga_ref, sgb_ref,
                   *, ck):
    x = x_ref[0]
    ms = jnp.mean(x * x, axis=-1, keepdims=True)
    h = x * lax.rsqrt(ms + EPS) * g_ref[...]
    h = h * (1.0 + sc_ref[0]) + sh_ref[0]
    hb = h.astype(bf16)

    def proj(w_ref):
        return jnp.dot(hb, w_ref[...], preferred_element_type=f32)

    ptT = proj(wt_ref).T
    qT_ref[0] = ptT[0:ATTN_WIDTH].astype(bf16)
    qiT_ref[0] = ptT[ATTN_WIDTH:2 * ATTN_WIDTH].astype(bf16)
    tm = x.shape[0]
    for t in range(tm // ck):
        vT_ref[0, t] = ptT[2 * ATTN_WIDTH:3 * ATTN_WIDTH, t * ck:(t + 1) * ck].astype(bf16)
    k_ref[0] = proj(wk_ref).astype(bf16)
    ki_ref[0] = proj(wki_ref).astype(bf16)
    wiT = proj(wwi_ref).T
    wT_ref[0] = wiT[0:IDX_HEADS] * (IDX_HEADS ** -0.5)
    z_ref[0] = proj(wua_ref) * jax.nn.sigmoid(proj(wug_ref))
    sga_ref[0] = jax.nn.sigmoid(proj(wga_ref)).astype(bf16)
    sgb_ref[0] = jax.nn.sigmoid(proj(wgb_ref)).astype(bf16)


def _inproj(x, shift, scale, g, w_in, *, tm, ck):
    B, S, D = x.shape
    W = ATTN_WIDTH
    o = 0
    wq = w_in[:, o:o + W]; o += W
    wk = w_in[:, o:o + W]; o += W
    wv = w_in[:, o:o + W]; o += W
    wqi = w_in[:, o:o + IDX_HEADS * IDX_DIM]; o += IDX_HEADS * IDX_DIM
    wki = w_in[:, o:o + IDX_DIM]; o += IDX_DIM
    wwi = w_in[:, o:o + IDX_HEADS]; o += IDX_HEADS
    wua = w_in[:, o:o + CONV_CH]; o += CONV_CH
    wug = w_in[:, o:o + CONV_CH]; o += CONV_CH
    wga = w_in[:, o:o + D]; o += D
    wgb = w_in[:, o:o + D]; o += D
    wt = jnp.concatenate([wq * (HEAD_DIM ** -0.5 * LOG2E), wqi * (IDX_DIM ** -0.5), wv], axis=1).astype(bf16)
    pad = lambda w: jnp.pad(w, ((0, 0), (0, LANES - w.shape[1]))).astype(bf16)
    weights = [wt, wk.astype(bf16), pad(wki), pad(wwi), wua.astype(bf16), wug.astype(bf16),
               wga.astype(bf16), wgb.astype(bf16)]

    tile = lambda n: pl.BlockSpec((1, tm, n), lambda b, i: (b, i, 0))
    tileT = lambda n: pl.BlockSpec((1, n, tm), lambda b, i: (b, 0, i))
    vec = pl.BlockSpec((1, 1, D), lambda b, i: (b, 0, 0))
    out_shape = [
        jax.ShapeDtypeStruct((B, W, S), bf16),
        jax.ShapeDtypeStruct((B, W, S), bf16),
        jax.ShapeDtypeStruct((B, S // ck, W, ck), bf16),
        jax.ShapeDtypeStruct((B, S, W), bf16),
        jax.ShapeDtypeStruct((B, S, LANES), bf16),
        jax.ShapeDtypeStruct((B, IDX_HEADS, S), f32),
        jax.ShapeDtypeStruct((B, S, CONV_CH), f32),
        jax.ShapeDtypeStruct((B, S, D), bf16),
        jax.ShapeDtypeStruct((B, S, D), bf16),
    ]
    out_specs = [tileT(W), tileT(W),
                 pl.BlockSpec((1, tm // ck, W, ck), lambda b, i: (b, i, 0, 0)),
                 tile(W), tile(LANES), tileT(IDX_HEADS), tile(CONV_CH), tile(D), tile(D)]
    in_specs = [tile(D), vec, vec, pl.BlockSpec((1, D), lambda b, i: (0, 0))]
    in_specs += [_resident(w.shape, lambda b, i: (0, 0)) for w in weights]
    return pl.pallas_call(
        functools.partial(_inproj_kernel, ck=ck),
        out_shape=out_shape,
        grid=(B, S // tm),
        in_specs=in_specs,
        out_specs=out_specs,
        compiler_params=pltpu.CompilerParams(dimension_semantics=("arbitrary", "arbitrary"),
                                             vmem_limit_bytes=VMEM_LIMIT),
        name="inproj",
    )(x, shift, scale, g.reshape(1, D), *weights)


def _ordered_bits_to_float(u):
    key = u ^ jnp.int32(-2 ** 31)
    bits = key ^ ((key >> 31) & jnp.int32(0x7FFFFFFF))
    return lax.bitcast_convert_type(bits, f32)


SLOPE_PIECES = 4


def _slope_pieces(h):
    rem = LOG2E * 2.0 ** (-8.0 * (h + 1) / N_HEADS)
    pieces = []
    for _ in range(SLOPE_PIECES):
        piece = float(np.asarray(rem, np.float32).astype(jnp.bfloat16).astype(np.float32))
        pieces.append(piece)
        rem -= piece
    return pieces


def _attn_kernel(qT_ref, qiT_ref, wT_ref, k_ref, vT_ref, ki_ref, o_ref,
                 sc_ref, sb_ref, qpad_ref, qipad_ref, m_ref, acc_ref, pos_ref, mb_ref, s_ref, stage_ref,
                 *, topk):
    tq = qT_ref.shape[2]
    ck = vT_ref.shape[3]
    cc = ATT_CC
    i = pl.program_id(1)
    q0 = i * tq
    n_keys = q0 + tq
    n_ck = (n_keys + ck - 1) // ck
    n_cc = (n_keys + cc - 1) // cc

    lane1 = lax.broadcasted_iota(jnp.int32, (1, tq), 1)

    @pl.when(i == 0)
    def _():
        for c in range(pos_ref.shape[0] // ck):
            kpos = c * ck + lax.broadcasted_iota(jnp.int32, (ck, LANES), 0)
            lane = lax.broadcasted_iota(jnp.int32, (ck, LANES), 1)
            feat = jnp.where(lane < SLOPE_PIECES, kpos >> 7,
                             jnp.where(lane < 2 * SLOPE_PIECES, kpos & (LANES - 1), 0))
            pos_ref[c * ck:(c + 1) * ck, :] = feat.astype(f32).astype(bf16)
        rowi = lax.broadcasted_iota(jnp.int32, (LANES, tq), 0)
        for h in range(N_HEADS):
            coef = jnp.zeros((LANES, tq), f32)
            for p, piece in enumerate(_slope_pieces(h)):
                coef = jnp.where(rowi == p, LANES * piece, coef)
                coef = jnp.where(rowi == SLOPE_PIECES + p, piece, coef)
            qpad_ref[h, 2 * HEAD_DIM:2 * HEAD_DIM + LANES, :] = coef.astype(bf16)

    qT = qT_ref[0]
    qiT = qiT_ref[0]
    first_half = lax.broadcasted_iota(jnp.int32, (2 * HEAD_DIM, tq), 0) < HEAD_DIM
    zeros_half = jnp.zeros((IDX_DIM, tq), bf16)
    for h in range(N_HEADS):
        pair = qT[(h // 2) * 2 * HEAD_DIM:(h // 2 + 1) * 2 * HEAD_DIM]
        keep = first_half if h % 2 == 0 else jnp.logical_not(first_half)
        qpad_ref[h, 0:2 * HEAD_DIM, :] = jnp.where(keep, pair, jnp.zeros_like(pair))
    for h in range(IDX_HEADS):
        qipad_ref[h] = jnp.concatenate([qiT[h * IDX_DIM:(h + 1) * IDX_DIM], zeros_half], axis=0)

    w_rows = wT_ref[0]
    row_t = lax.broadcasted_iota(jnp.int32, (tq, tq), 0)
    lane_t = lax.broadcasted_iota(jnp.int32, (tq, tq), 1)

    def score_chunk(j, carry):
        for t in range(ck // tq):
            k0 = pl.multiple_of(j * ck + t * tq, tq)
            kit = ki_ref[0, pl.ds(k0, tq), :]
            acc = jnp.zeros((tq, tq), f32)
            for h in range(IDX_HEADS):
                lg = jnp.dot(kit, qipad_ref[h], preferred_element_type=f32)
                acc = acc + w_rows[h:h + 1, :] * jnp.maximum(lg, 0.0)
            causal = (k0 + row_t) <= (q0 + lane_t)
            acc = jnp.where(causal, acc, -jnp.inf)
            sc_ref[pl.ds(k0, tq), :] = acc
            sb_ref[pl.ds(k0, tq), :] = acc.astype(coarse_t)
        return carry

    lax.fori_loop(0, n_ck, score_chunk, 0)

    @pl.when(n_ck * ck < n_cc * cc)
    def _():
        for r0 in range(ck, cc, ck):
            start = pl.multiple_of(n_ck * ck + (r0 - ck), ck)
            sc_ref[pl.ds(start, ck), :] = jnp.full((ck, tq), -jnp.inf, f32)
            sb_ref[pl.ds(start, ck), :] = jnp.full((ck, tq), -jnp.inf, coarse_t)

    kf = jnp.minimum(topk, q0 + lane1 + 1).astype(f32)
    row_c = lax.broadcasted_iota(jnp.int32, (cc, tq), 0)

    def count(pred):
        def body(j, acc):
            k0 = pl.multiple_of(j * cc, cc)
            s = sc_ref[pl.ds(k0, cc), :]
            hit = jnp.where(pred(s, k0 + row_c), 1.0, 0.0)
            return acc + _reduce_rows(jnp.add, hit, SUBLANES)
        acc = lax.fori_loop(0, n_cc, body, jnp.zeros((SUBLANES, tq), f32))
        return acc.sum(axis=0, keepdims=True)

    def count_coarse(thr16):
        one = jnp.ones((), coarse_t)
        zero = jnp.zeros((), coarse_t)

        def body(j, acc):
            k0 = pl.multiple_of(j * cc, cc)
            hit = jnp.where(sb_ref[pl.ds(k0, cc), :] >= thr16, one, zero)
            return acc + _reduce_rows(jnp.add, hit, PACKED_ROWS).astype(f32)
        acc = lax.fori_loop(0, n_cc, body, jnp.zeros((PACKED_ROWS, tq), f32))
        return acc.sum(axis=0, keepdims=True)

    def coarse_value_bits(v):
        return (v ^ jnp.where((v & 0x8000) != 0, 0x8000, 0xFFFF)) << 16

    def bisect_coarse(it, v):
        cand = v | lax.shift_left(jnp.int32(1), 15 - it)
        thr16 = lax.bitcast_convert_type(coarse_value_bits(cand), f32).astype(coarse_t)
        return jnp.where(count_coarse(thr16) >= kf, cand, v)

    v = lax.fori_loop(0, 16, bisect_coarse, jnp.zeros((1, tq), jnp.int32))
    vbits = coarse_value_bits(v)
    u_mid = vbits ^ jnp.where(vbits < 0, jnp.int32(-1), jnp.int32(-2 ** 31))
    u_base = u_mid - 0x8000
    fine_bits = 17

    def fine_body(it, c):
        d, done = c
        cand = d | lax.shift_left(jnp.int32(1), fine_bits - 1 - it)
        thr_c = _ordered_bits_to_float(u_base + cand)
        n = count(lambda s, kpos: s >= thr_c)
        d = jnp.where(jnp.logical_and(done < 0.5, n >= kf), cand, d)
        done = jnp.maximum(done, jnp.where(n == kf, 1.0, 0.0))
        return d, done

    def all_done(c):
        return jnp.min(c[1]) > 0.5

    c = lax.fori_loop(0, FINE_CHECKS[0], fine_body,
                      (jnp.zeros((1, tq), jnp.int32), jnp.zeros((1, tq), f32)))
    for lo, hi in zip(FINE_CHECKS, FINE_CHECKS[1:] + (fine_bits,)):
        c = lax.cond(all_done(c), lambda c: c,
                     functools.partial(lax.fori_loop, lo, hi, fine_body), c)
    d = c[0]
    u = u_base + d
    thr = _ordered_bits_to_float(u)
    thr_next = _ordered_bits_to_float(u + 1)

    @pl.when(jnp.logical_not(all_done(c)))
    def _():
        need = kf - count(lambda s, kpos: s >= thr_next)

        def bisect_index(it, jcut):
            cand = jcut | lax.shift_left(jnp.int32(1), 13 - it)
            c = count(lambda s, kpos: (s >= thr) & (s < thr_next) & (kpos < cand))
            return jnp.where(c <= need, cand, jcut)

        jcut = lax.fori_loop(0, 14, bisect_index, jnp.zeros((1, tq), jnp.int32))

        def drop_chunk(j, carry):
            k0 = pl.multiple_of(j * cc, cc)
            s = sc_ref[pl.ds(k0, cc), :]
            drop = (s >= thr) & (s < thr_next) & ((k0 + row_c) >= jcut)
            sc_ref[pl.ds(k0, cc), :] = jnp.where(drop, -jnp.inf, s)
            return carry

        lax.fori_loop(0, n_cc, drop_chunk, 0)

    m_ref[...] = jnp.full(m_ref.shape, NEG_BIG, f32)
    acc_ref[...] = jnp.zeros(acc_ref.shape, f32)

    ones_rows = jnp.ones((SUM_ROWS, ck), bf16)

    def logits_step(j, buf):
        k0 = pl.multiple_of(j * ck, ck)
        mb_ref[buf] = jnp.where(sc_ref[pl.ds(k0, ck), :] >= thr, 0.0, NEG_BIG)
        pos = pos_ref[pl.ds(k0, ck), :]
        for h in range(N_HEADS):
            kp = k_ref[0, pl.ds(k0, ck), (h // 2) * LANES:(h // 2 + 1) * LANES]
            s = jnp.dot(jnp.concatenate([kp, pos], axis=1), qpad_ref[h],
                        preferred_element_type=f32)
            s = s + mb_ref[buf]
            s_ref[buf, h] = s
            mt = _reduce_rows(jnp.maximum, s, SUBLANES).max(axis=0, keepdims=True)
            m_old = m_ref[h:h + 1, :]
            m_new = jnp.maximum(m_old, mt)
            m_ref[h:h + 1, :] = m_new
            stage_ref[buf, 0, h:h + 1, :] = m_new
            stage_ref[buf, 1, h:h + 1, :] = jnp.exp2(m_old - m_new)

    def probs_step(j, buf):
        for h in range(N_HEADS):
            p = jnp.exp2(s_ref[buf, h] - stage_ref[buf, 0, h:h + 1, :]).astype(bf16)
            vt = jnp.concatenate([vT_ref[0, j, h * HEAD_DIM:(h + 1) * HEAD_DIM, :], ones_rows], axis=0)
            pv = jnp.dot(vt, p, preferred_element_type=f32)
            acc_ref[h] = stage_ref[buf, 1, h:h + 1, :] * acc_ref[h] + pv

    logits_step(0, 0)

    def chunk_pair(t, carry):
        probs_step(2 * t, 0)
        logits_step(2 * t + 1, 1)
        probs_step(2 * t + 1, 1)
        logits_step(2 * t + 2, 0)
        return carry

    n_pairs = (n_ck - 1) // 2
    lax.fori_loop(0, n_pairs, chunk_pair, 0)
    probs_step(2 * n_pairs, 0)

    @pl.when(2 * n_pairs + 1 < n_ck)
    def _():
        logits_step(n_ck - 1, 1)
        probs_step(n_ck - 1, 1)

    outs = []
    for h in range(N_HEADS):
        outs.append(acc_ref[h, 0:HEAD_DIM, :] / acc_ref[h, HEAD_DIM:HEAD_DIM + 1, :])
    o_ref[0] = jnp.concatenate(outs, axis=0).T.astype(bf16)


def _attention(qT, qiT, wT, k, vT, ki, *, topk):
    B, W, S = qT.shape
    tq, ck, cc = ATT_TQ, ATT_CK, ATT_CC
    assert S % cc == 0 and cc == 2 * ck and ck % tq == 0 and vT.shape[3] == ck
    blkT = lambda n: pl.BlockSpec((1, n, tq), lambda b, i: (b, 0, i))
    return pl.pallas_call(
        functools.partial(_attn_kernel, topk=topk),
        out_shape=jax.ShapeDtypeStruct((B, S, W), bf16),
        grid=(B, S // tq),
        in_specs=[blkT(W), blkT(W), blkT(IDX_HEADS),
                  _resident((1, S, W), lambda b, i: (b, 0, 0)),
                  _resident((1, S // ck, W, ck), lambda b, i: (b, 0, 0, 0)),
                  _resident((1, S, LANES), lambda b, i: (b, 0, 0))],
        out_specs=pl.BlockSpec((1, tq, W), lambda b, i: (b, i, 0)),
        scratch_shapes=[pltpu.VMEM((S, tq), f32),
                        pltpu.VMEM((S, tq), coarse_t),
                        pltpu.VMEM((N_HEADS, 2 * HEAD_DIM + LANES, tq), bf16),
                        pltpu.VMEM((IDX_HEADS, 2 * IDX_DIM, tq), bf16),
                        pltpu.VMEM((N_HEADS, tq), f32),
                        pltpu.VMEM((N_HEADS, HEAD_DIM + SUM_ROWS, tq), f32),
                        pltpu.VMEM((S, LANES), bf16),
                        pltpu.VMEM((2, ck, tq), f32),
                        pltpu.VMEM((2, N_HEADS, ck, tq), f32),
                        pltpu.VMEM((2, 2, N_HEADS, tq), f32)],
        compiler_params=pltpu.CompilerParams(dimension_semantics=("arbitrary", "arbitrary"),
                                             vmem_limit_bytes=VMEM_LIMIT),
        name="dsa_attn",
    )(qT, qiT, wT, k, vT, ki)


def _conv_kernel(zc_ref, zp_ref, w_ref, b_ref, g_ref, beta_ref, o_ref, ext_ref):
    tm = zc_ref.shape[1]
    i = pl.program_id(1)
    prev = jnp.where(i > 0, zp_ref[0], 0.0)
    ext_ref[0:CONV_HALO, :] = prev
    ext_ref[CONV_HALO:CONV_HALO + tm, :] = zc_ref[0]
    n_ext = CONV_HALO + tm
    ext = ext_ref[...]
    acc = jnp.broadcast_to(b_ref[...], (tm, CONV_CH))
    first = CONV_HALO - (CONV_WIDTH - 1)
    for b in range(SUBLANES):
        offs = [o for o in range(first, first + CONV_WIDTH) if o % SUBLANES == b]
        if not offs:
            continue
        shifted = ext if b == 0 else pltpu.roll(ext, n_ext - b, axis=0)
        for o in offs:
            a = o - b
            acc = acc + w_ref[o - first:o - first + 1, :] * shifted[a:a + tm]
    mu = jnp.mean(acc, axis=-1, keepdims=True)
    d = acc - mu
    var = jnp.mean(d * d, axis=-1, keepdims=True)
    y = d * lax.rsqrt(var + EPS) * g_ref[...] + beta_ref[...]
    o_ref[0] = (y * jax.nn.sigmoid(y)).astype(bf16)


def _conv(z, w_dw, b_dw, ln_g, ln_b, *, tm):
    B, S, C = z.shape
    r = tm // CONV_HALO
    vec = pl.BlockSpec((1, C), lambda b, i: (0, 0))
    return pl.pallas_call(
        _conv_kernel,
        out_shape=jax.ShapeDtypeStruct((B, S, C), bf16),
        grid=(B, S // tm),
        in_specs=[pl.BlockSpec((1, tm, C), lambda b, i: (b, i, 0)),
                  pl.BlockSpec((1, CONV_HALO, C), lambda b, i: (b, jnp.maximum(i * r - 1, 0), 0)),
                  pl.BlockSpec((CONV_WIDTH, C), lambda b, i: (0, 0)),
                  vec, vec, vec],
        out_specs=pl.BlockSpec((1, tm, C), lambda b, i: (b, i, 0)),
        scratch_shapes=[pltpu.VMEM((CONV_HALO + tm, C), f32)],
        compiler_params=pltpu.CompilerParams(dimension_semantics=("arbitrary", "arbitrary")),
        name="conv",
    )(z, z, w_dw.reshape(CONV_WIDTH, C), b_dw.reshape(1, C), ln_g.reshape(1, C), ln_b.reshape(1, C))


def _post_kernel(x_ref, attn_ref, conv_ref, sga_ref, sgb_ref, gm_ref, shf_ref, scf_ref, gf_ref,
                 gffn_ref, gfin_ref, wap_ref, wcp_ref, wout_ref, wfa_ref, wfb_ref, wfo_ref, o_ref,
                 *, final_norm):
    dot = lambda a, w_ref: jnp.dot(a, w_ref[...], preferred_element_type=f32)
    ya = dot(attn_ref[0], wap_ref)
    yc = dot(conv_ref[0], wcp_ref)
    merged = sga_ref[0].astype(f32) * ya + sgb_ref[0].astype(f32) * yc
    x1 = x_ref[0] + gm_ref[0] * dot(merged.astype(bf16), wout_ref)

    ms = jnp.mean(x1 * x1, axis=-1, keepdims=True)
    h = x1 * lax.rsqrt(ms + EPS) * gffn_ref[...]
    hb = (h * (1.0 + scf_ref[0]) + shf_ref[0]).astype(bf16)
    a = dot(hb, wfa_ref)
    b = dot(hb, wfb_ref)
    act = (a * jax.nn.sigmoid(a) * b).astype(bf16)
    x2 = x1 + gf_ref[0] * dot(act, wfo_ref)

    if final_norm:
        ms2 = jnp.mean(x2 * x2, axis=-1, keepdims=True)
        x2 = x2 * lax.rsqrt(ms2 + EPS) * gfin_ref[...]
    o_ref[0] = x2


def _post(x, attn, conv, sga, sgb, g_m, sh_f, sc_f, g_f, norm_ffn_g, norm_final_g,
          w_attn_proj, w_conv_proj, w_out, w_ffn_in, w_ffn_out, *, tm, final_norm):
    B, S, D = x.shape
    d_ff = w_ffn_out.shape[0]
    weights = [w_attn_proj.astype(bf16), w_conv_proj.astype(bf16), w_out.astype(bf16),
               w_ffn_in[:, :d_ff].astype(bf16), w_ffn_in[:, d_ff:].astype(bf16), w_ffn_out.astype(bf16)]
    tile = lambda n: pl.BlockSpec((1, tm, n), lambda b, i: (b, i, 0))
    vec = pl.BlockSpec((1, 1, D), lambda b, i: (b, 0, 0))
    gain = pl.BlockSpec((1, D), lambda b, i: (0, 0))
    in_specs = [tile(D), tile(ATTN_WIDTH), tile(CONV_CH), tile(D), tile(D), vec, vec, vec, vec, gain, gain]
    in_specs += [_resident(w.shape, lambda b, i: (0, 0)) for w in weights]
    return pl.pallas_call(
        functools.partial(_post_kernel, final_norm=final_norm),
        out_shape=jax.ShapeDtypeStruct((B, S, D), f32),
        grid=(B, S // tm),
        in_specs=in_specs,
        out_specs=tile(D),
        compiler_params=pltpu.CompilerParams(dimension_semantics=("arbitrary", "arbitrary"),
                                             vmem_limit_bytes=VMEM_LIMIT),
        name="post",
    )(x, attn, conv, sga, sgb, g_m, sh_f, sc_f, g_f, norm_ffn_g.reshape(1, D),
      norm_final_g.reshape(1, D), *weights)


def kernel(x, c, norm_mix_g, w_in, w_dw, b_dw, conv_ln_g, conv_ln_b, w_attn_proj, w_conv_proj,
           w_out, norm_ffn_g, w_ffn_in, w_ffn_out, w_ada, b_ada, norm_final_g):
    B, S, D = x.shape
    depth = w_in.shape[0]
    topk = min(TOPK_MAX, S // 4)
    c = c.astype(f32)
    for l in range(depth):
        mod = _ada(c, w_ada[l], b_ada[l])
        sh_m, sc_m, g_m, sh_f, sc_f, g_f = [m.reshape(B, 1, D) for m in jnp.split(mod, N_MOD, axis=-1)]
        qT, qiT, vT, k, ki, wT, z, sga, sgb = _inproj(x, sh_m, sc_m, norm_mix_g[l], w_in[l],
                                                      tm=512, ck=ATT_CK)
        attn = _attention(qT, qiT, wT, k, vT, ki, topk=topk)
        conv = _conv(z, w_dw[l], b_dw[l], conv_ln_g[l], conv_ln_b[l], tm=512)
        x = _post(x, attn, conv, sga, sgb, g_m, sh_f, sc_f, g_f, norm_ffn_g[l], norm_final_g,
                  w_attn_proj[l], w_conv_proj[l], w_out[l], w_ffn_in[l], w_ffn_out[l], tm=256,
                  final_norm=(l == depth - 1))
    return x
```

```python
import functools
import math

import jax
import jax.numpy as jnp
import numpy as np
from jax import lax
from jax.experimental import pallas as pl
from jax.experimental.pallas import tpu as pltpu

N_HEADS = 8
HEAD_DIM = 64
ATTN_WIDTH = N_HEADS * HEAD_DIM
IDX_HEADS = 8
IDX_DIM = 64
TOPK_MAX = 256
CONV_CH = 512
CONV_WIDTH = 31
N_MOD = 6
EPS = 1e-6

LANES = 128
SUBLANES = 8
PACKED_ROWS = 16
CONV_HALO = 32
NEG_BIG = -1e30
VMEM_LIMIT = 56 * 1024 * 1024

ATT_TQ = 256
ATT_CK = 512
ATT_CC = 1024
SUM_ROWS = 16
FINE_CHECKS = (12, 14)
LOG2E = math.log2(math.e)

f32 = jnp.float32
bf16 = jnp.bfloat16
coarse_t = jnp.bfloat16


def _resident(block_shape, index_map):
    return pl.BlockSpec(block_shape, index_map, pipeline_mode=pl.Buffered(1))


REDUCE_CHAINS = 2


def _reduce_rows(op, x, rows):
    groups = [x[r * rows:(r + 1) * rows] for r in range(x.shape[0] // rows)]
    chains = groups[:REDUCE_CHAINS]
    for r, g in enumerate(groups[REDUCE_CHAINS:]):
        chains[r % REDUCE_CHAINS] = op(chains[r % REDUCE_CHAINS], g)
    while len(chains) > 1:
        chains = [op(chains[a], chains[a + 1]) for a in range(0, len(chains), 2)]
    return chains[0]


def _ada_kernel(c_ref, w_ref, b_ref, o_ref):
    c = c_ref[...]
    ca = c * jax.nn.sigmoid(c)
    o_ref[...] = jnp.dot(ca, w_ref[...], preferred_element_type=f32,
                         precision=lax.Precision.HIGHEST) + b_ref[...]


def _ada(c, w_ada, b_ada):
    B, D = c.shape
    N = w_ada.shape[1]
    rows = SUBLANES
    c_pad = jnp.zeros((rows, D), f32).at[:B].set(c)
    tn = 1024
    out = pl.pallas_call(
        _ada_kernel,
        out_shape=jax.ShapeDtypeStruct((rows, N), f32),
        grid=(N // tn,),
        in_specs=[pl.BlockSpec((rows, D), lambda j: (0, 0)),
                  pl.BlockSpec((D, tn), lambda j: (0, j)),
                  pl.BlockSpec((1, tn), lambda j: (0, j))],
        out_specs=pl.BlockSpec((rows, tn), lambda j: (0, j)),
        compiler_params=pltpu.CompilerParams(dimension_semantics=("arbitrary",)),
        name="ada",
    )(c_pad, w_ada, b_ada.reshape(1, N))
    return out[:B]


def _inproj_kernel(x_ref, sh_ref, sc_ref, g_ref, wt_ref, wk_ref, wki_ref, wwi_ref,
                   wua_ref, wug_ref, wga_ref, wgb_ref,
                   qT_ref, qiT_ref, vT_ref, k_ref, ki_ref, wT_ref, z_ref, sga_ref, sgb_ref,
                   *, ck):
    x = x_ref[0]
    ms = jnp.mean(x * x, axis=-1, keepdims=True)
    h = x * lax.rsqrt(ms + EPS) * g_ref[...]
    h = h * (1.0 + sc_ref[0]) + sh_ref[0]
    hb = h.astype(bf16)

    def proj(w_ref):
        return jnp.dot(hb, w_ref[...], preferred_element_type=f32)

    ptT = proj(wt_ref).T
    qT_ref[0] = ptT[0:ATTN_WIDTH].astype(bf16)
    qiT_ref[0] = ptT[ATTN_WIDTH:2 * ATTN_WIDTH].astype(bf16)
    tm = x.shape[0]
    for t in range(tm // ck):
        vT_ref[0, t] = ptT[2 * ATTN_WIDTH:3 * ATTN_WIDTH, t * ck:(t + 1) * ck].astype(bf16)
    k_ref[0] = proj(wk_ref).astype(bf16)
    ki_ref[0] = proj(wki_ref).astype(bf16)
    wiT = proj(wwi_ref).T
    wT_ref[0] = wiT[0:IDX_HEADS] * (IDX_HEADS ** -0.5)
    z_ref[0] = proj(wua_ref) * jax.nn.sigmoid(proj(wug_ref))
    sga_ref[0] = jax.nn.sigmoid(proj(wga_ref)).astype(bf16)
    sgb_ref[0] = jax.nn.sigmoid(proj(wgb_ref)).astype(bf16)


def _inproj(x, shift, scale, g, w_in, *, tm, ck):
    B, S, D = x.shape
    W = ATTN_WIDTH
    o = 0
    wq = w_in[:, o:o + W]; o += W
    wk = w_in[:, o:o + W]; o += W
    wv = w_in[:, o:o + W]; o += W
    wqi = w_in[:, o:o + IDX_HEADS * IDX_DIM]; o += IDX_HEADS * IDX_DIM
    wki = w_in[:, o:o + IDX_DIM]; o += IDX_DIM
    wwi = w_in[:, o:o + IDX_HEADS]; o += IDX_HEADS
    wua = w_in[:, o:o + CONV_CH]; o += CONV_CH
    wug = w_in[:, o:o + CONV_CH]; o += CONV_CH
    wga = w_in[:, o:o + D]; o += D
    wgb = w_in[:, o:o + D]; o += D
    wt = jnp.concatenate([wq * (HEAD_DIM ** -0.5 * LOG2E), wqi * (IDX_DIM ** -0.5), wv], axis=1).astype(bf16)
    pad = lambda w: jnp.pad(w, ((0, 0), (0, LANES - w.shape[1]))).astype(bf16)
    weights = [wt, wk.astype(bf16), pad(wki), pad(wwi), wua.astype(bf16), wug.astype(bf16),
               wga.astype(bf16), wgb.astype(bf16)]

    tile = lambda n: pl.BlockSpec((1, tm, n), lambda b, i: (b, i, 0))
    tileT = lambda n: pl.BlockSpec((1, n, tm), lambda b, i: (b, 0, i))
    vec = pl.BlockSpec((1, 1, D), lambda b, i: (b, 0, 0))
    out_shape = [
        jax.ShapeDtypeStruct((B, W, S), bf16),
        jax.ShapeDtypeStruct((B, W, S), bf16),
        jax.ShapeDtypeStruct((B, S // ck, W, ck), bf16),
        jax.ShapeDtypeStruct((B, S, W), bf16),
        jax.ShapeDtypeStruct((B, S, LANES), bf16),
        jax.ShapeDtypeStruct((B, IDX_HEADS, S), f32),
        jax.ShapeDtypeStruct((B, S, CONV_CH), f32),
        jax.ShapeDtypeStruct((B, S, D), bf16),
        jax.ShapeDtypeStruct((B, S, D), bf16),
    ]
    out_specs = [tileT(W), tileT(W),
                 pl.BlockSpec((1, tm // ck, W, ck), lambda b, i: (b, i, 0, 0)),
                 tile(W), tile(LANES), tileT(IDX_HEADS), tile(CONV_CH), tile(D), tile(D)]
    in_specs = [tile(D), vec, vec, pl.BlockSpec((1, D), lambda b, i: (0, 0))]
    in_specs += [_resident(w.shape, lambda b, i: (0, 0)) for w in weights]
    return pl.pallas_call(
        functools.partial(_inproj_kernel, ck=ck),
        out_shape=out_shape,
        grid=(B, S // tm),
        in_specs=in_specs,
        out_specs=out_specs,
        compiler_params=pltpu.CompilerParams(dimension_semantics=("arbitrary", "arbitrary"),
                                             vmem_limit_bytes=VMEM_LIMIT),
        name="inproj",
    )(x, shift, scale, g.reshape(1, D), *weights)


def _ordered_bits_to_float(u):
    key = u ^ jnp.int32(-2 ** 31)
    bits = key ^ ((key >> 31) & jnp.int32(0x7FFFFFFF))
    return lax.bitcast_convert_type(bits, f32)


SLOPE_PIECES = 4


def _slope_pieces(h):
    rem = LOG2E * 2.0 ** (-8.0 * (h + 1) / N_HEADS)
    pieces = []
    for _ in range(SLOPE_PIECES):
        piece = float(np.asarray(rem, np.float32).astype(jnp.bfloat16).astype(np.float32))
        pieces.append(piece)
        rem -= piece
    return pieces


def _attn_kernel(qT_ref, qiT_ref, wT_ref, k_ref, vT_ref, ki_ref, o_ref,
                 sc_ref, sb_ref, qpad_ref, qipad_ref, m_ref, acc_ref, pos_ref, mb_ref, s_ref, stage_ref,
                 tri_ref, *, topk):
    tq = qT_ref.shape[2]
    ck = vT_ref.shape[3]
    cc = ATT_CC
    i = pl.program_id(1)
    q0 = i * tq
    n_keys = q0 + tq
    n_ck = (n_keys + ck - 1) // ck
    n_cc = (n_keys + cc - 1) // cc

    lane1 = lax.broadcasted_iota(jnp.int32, (1, tq), 1)

    @pl.when(i == 0)
    def _():
        for c in range(pos_ref.shape[0] // ck):
            kpos = c * ck + lax.broadcasted_iota(jnp.int32, (ck, LANES), 0)
            lane = lax.broadcasted_iota(jnp.int32, (ck, LANES), 1)
            feat = jnp.where(lane < SLOPE_PIECES, kpos >> 7,
                             jnp.where(lane < 2 * SLOPE_PIECES, kpos & (LANES - 1), 0))
            pos_ref[c * ck:(c + 1) * ck, :] = feat.astype(f32).astype(bf16)
        rowi = lax.broadcasted_iota(jnp.int32, (LANES, tq), 0)
        for h in range(N_HEADS):
            coef = jnp.zeros((LANES, tq), f32)
            for p, piece in enumerate(_slope_pieces(h)):
                coef = jnp.where(rowi == p, LANES * piece, coef)
                coef = jnp.where(rowi == SLOPE_PIECES + p, piece, coef)
            qpad_ref[h, 2 * HEAD_DIM:2 * HEAD_DIM + LANES, :] = coef.astype(bf16)
        tri_ref[...] = jnp.where(lax.broadcasted_iota(jnp.int32, (cc, cc), 1)
                                 < lax.broadcasted_iota(jnp.int32, (cc, cc), 0), 1.0, 0.0).astype(bf16)

    qT = qT_ref[0]
    qiT = qiT_ref[0]
    first_half = lax.broadcasted_iota(jnp.int32, (2 * HEAD_DIM, tq), 0) < HEAD_DIM
    zeros_half = jnp.zeros((IDX_DIM, tq), bf16)
    for h in range(N_HEADS):
        pair = qT[(h // 2) * 2 * HEAD_DIM:(h // 2 + 1) * 2 * HEAD_DIM]
        keep = first_half if h % 2 == 0 else jnp.logical_not(first_half)
        qpad_ref[h, 0:2 * HEAD_DIM, :] = jnp.where(keep, pair, jnp.zeros_like(pair))
    for h in range(IDX_HEADS):
        qipad_ref[h] = jnp.concatenate([qiT[h * IDX_DIM:(h + 1) * IDX_DIM], zeros_half], axis=0)

    w_rows = wT_ref[0]
    row_t = lax.broadcasted_iota(jnp.int32, (tq, tq), 0)
    lane_t = lax.broadcasted_iota(jnp.int32, (tq, tq), 1)

    def score_chunk(j, carry):
        for t in range(ck // tq):
            k0 = pl.multiple_of(j * ck + t * tq, tq)
            kit = ki_ref[0, pl.ds(k0, tq), :]
            acc = jnp.zeros((tq, tq), f32)
            for h in range(IDX_HEADS):
                lg = jnp.dot(kit, qipad_ref[h], preferred_element_type=f32)
                acc = acc + w_rows[h:h + 1, :] * jnp.maximum(lg, 0.0)
            causal = (k0 + row_t) <= (q0 + lane_t)
            acc = jnp.where(causal, acc, -jnp.inf)
            sc_ref[pl.ds(k0, tq), :] = acc
            sb_ref[pl.ds(k0, tq), :] = acc.astype(coarse_t)
        return carry

    lax.fori_loop(0, n_ck, score_chunk, 0)

    @pl.when(n_ck * ck < n_cc * cc)
    def _():
        for r0 in range(ck, cc, ck):
            start = pl.multiple_of(n_ck * ck + (r0 - ck), ck)
            sc_ref[pl.ds(start, ck), :] = jnp.full((ck, tq), -jnp.inf, f32)
            sb_ref[pl.ds(start, ck), :] = jnp.full((ck, tq), -jnp.inf, coarse_t)

    kf = jnp.minimum(topk, q0 + lane1 + 1).astype(f32)

    def count(pred):
        def body(j, acc):
            k0 = pl.multiple_of(j * cc, cc)
            hit = jnp.where(pred(sc_ref[pl.ds(k0, cc), :]), 1.0, 0.0)
            return acc + _reduce_rows(jnp.add, hit, SUBLANES)
        acc = lax.fori_loop(0, n_cc, body, jnp.zeros((SUBLANES, tq), f32))
        return acc.sum(axis=0, keepdims=True)

    def count_coarse(thr16):
        one = jnp.ones((), coarse_t)
        zero = jnp.zeros((), coarse_t)

        def body(j, acc):
            k0 = pl.multiple_of(j * cc, cc)
            hit = jnp.where(sb_ref[pl.ds(k0, cc), :] >= thr16, one, zero)
            return acc + _reduce_rows(jnp.add, hit, PACKED_ROWS).astype(f32)
        acc = lax.fori_loop(0, n_cc, body, jnp.zeros((PACKED_ROWS, tq), f32))
        return acc.sum(axis=0, keepdims=True)

    def coarse_value_bits(v):
        return (v ^ jnp.where((v & 0x8000) != 0, 0x8000, 0xFFFF)) << 16

    def bisect_coarse(it, v):
        cand = v | lax.shift_left(jnp.int32(1), 15 - it)
        thr16 = lax.bitcast_convert_type(coarse_value_bits(cand), f32).astype(coarse_t)
        return jnp.where(count_coarse(thr16) >= kf, cand, v)

    v = lax.fori_loop(0, 16, bisect_coarse, jnp.zeros((1, tq), jnp.int32))
    vbits = coarse_value_bits(v)
    u_mid = vbits ^ jnp.where(vbits < 0, jnp.int32(-1), jnp.int32(-2 ** 31))
    u_base = u_mid - 0x8000
    fine_bits = 17

    def fine_body(it, c):
        d, done = c
        cand = d | lax.shift_left(jnp.int32(1), fine_bits - 1 - it)
        thr_c = _ordered_bits_to_float(u_base + cand)
        n = count(lambda s: s >= thr_c)
        d = jnp.where(jnp.logical_and(done < 0.5, n >= kf), cand, d)
        done = jnp.maximum(done, jnp.where(n == kf, 1.0, 0.0))
        return d, done

    def all_done(c):
        return jnp.min(c[1]) > 0.5

    c = lax.fori_loop(0, FINE_CHECKS[0], fine_body,
                      (jnp.zeros((1, tq), jnp.int32), jnp.zeros((1, tq), f32)))
    for lo, hi in zip(FINE_CHECKS, FINE_CHECKS[1:] + (fine_bits,)):
        c = lax.cond(all_done(c), lambda c: c,
                     functools.partial(lax.fori_loop, lo, hi, fine_body), c)
    d = c[0]
    u = u_base + d
    thr = _ordered_bits_to_float(u)
    thr_next = _ordered_bits_to_float(u + 1)

    @pl.when(jnp.logical_not(all_done(c)))
    def _():
        need = kf - count(lambda s: s >= thr_next)

        def drop_chunk(j, before):
            k0 = pl.multiple_of(j * cc, cc)
            s = sc_ref[pl.ds(k0, cc), :]
            tie = (s >= thr) & (s < thr_next)
            tie01 = jnp.where(tie, 1.0, 0.0)
            rank = before + jnp.dot(tri_ref[...], tie01.astype(bf16), preferred_element_type=f32)
            sc_ref[pl.ds(k0, cc), :] = jnp.where(tie & (rank >= need), -jnp.inf, s)
            return before + _reduce_rows(jnp.add, tie01, SUBLANES).sum(axis=0, keepdims=True)

        lax.fori_loop(0, n_cc, drop_chunk, jnp.zeros((1, tq), f32))

    m_ref[...] = jnp.full(m_ref.shape, NEG_BIG, f32)
    acc_ref[...] = jnp.zeros(acc_ref.shape, f32)

    ones_rows = jnp.ones((SUM_ROWS, ck), bf16)

    def logits_step(j, buf):
        k0 = pl.multiple_of(j * ck, ck)
        mb_ref[buf] = jnp.where(sc_ref[pl.ds(k0, ck), :] >= thr, 0.0, NEG_BIG)
        pos = pos_ref[pl.ds(k0, ck), :]
        for h in range(N_HEADS):
            kp = k_ref[0, pl.ds(k0, ck), (h // 2) * LANES:(h // 2 + 1) * LANES]
            s = jnp.dot(jnp.concatenate([kp, pos], axis=1), qpad_ref[h],
                        preferred_element_type=f32)
            s = s + mb_ref[buf]
            s_ref[buf, h] = s
            mt = _reduce_rows(jnp.maximum, s, SUBLANES).max(axis=0, keepdims=True)
            m_old = m_ref[h:h + 1, :]
            m_new = jnp.maximum(m_old, mt)
            m_ref[h:h + 1, :] = m_new
            stage_ref[buf, 0, h:h + 1, :] = m_new
            stage_ref[buf, 1, h:h + 1, :] = jnp.exp2(m_old - m_new)

    def probs_step(j, buf):
        for h in range(N_HEADS):
            p = jnp.exp2(s_ref[buf, h] - stage_ref[buf, 0, h:h + 1, :]).astype(bf16)
            vt = jnp.concatenate([vT_ref[0, j, h * HEAD_DIM:(h + 1) * HEAD_DIM, :], ones_rows], axis=0)
            pv = jnp.dot(vt, p, preferred_element_type=f32)
            acc_ref[h] = stage_ref[buf, 1, h:h + 1, :] * acc_ref[h] + pv

    logits_step(0, 0)

    def chunk_pair(t, carry):
        probs_step(2 * t, 0)
        logits_step(2 * t + 1, 1)
        probs_step(2 * t + 1, 1)
        logits_step(2 * t + 2, 0)
        return carry

    n_pairs = (n_ck - 1) // 2
    lax.fori_loop(0, n_pairs, chunk_pair, 0)
    probs_step(2 * n_pairs, 0)

    @pl.when(2 * n_pairs + 1 < n_ck)
    def _():
        logits_step(n_ck - 1, 1)
        probs_step(n_ck - 1, 1)

    outs = []
    for h in range(N_HEADS):
        outs.append(acc_ref[h, 0:HEAD_DIM, :] / acc_ref[h, HEAD_DIM:HEAD_DIM + 1, :])
    o_ref[0] = jnp.concatenate(outs, axis=0).T.astype(bf16)


def _attention(qT, qiT, wT, k, vT, ki, *, topk):
    B, W, S = qT.shape
    tq, ck, cc = ATT_TQ, ATT_CK, ATT_CC
    assert S % cc == 0 and cc == 2 * ck and ck % tq == 0 and vT.shape[3] == ck
    blkT = lambda n: pl.BlockSpec((1, n, tq), lambda b, i: (b, 0, i))
    return pl.pallas_call(
        functools.partial(_attn_kernel, topk=topk),
        out_shape=jax.ShapeDtypeStruct((B, S, W), bf16),
        grid=(B, S // tq),
        in_specs=[blkT(W), blkT(W), blkT(IDX_HEADS),
                  _resident((1, S, W), lambda b, i: (b, 0, 0)),
                  _resident((1, S // ck, W, ck), lambda b, i: (b, 0, 0, 0)),
                  _resident((1, S, LANES), lambda b, i: (b, 0, 0))],
        out_specs=pl.BlockSpec((1, tq, W), lambda b, i: (b, i, 0)),
        scratch_shapes=[pltpu.VMEM((S, tq), f32),
                        pltpu.VMEM((S, tq), coarse_t),
                        pltpu.VMEM((N_HEADS, 2 * HEAD_DIM + LANES, tq), bf16),
                        pltpu.VMEM((IDX_HEADS, 2 * IDX_DIM, tq), bf16),
                        pltpu.VMEM((N_HEADS, tq), f32),
                        pltpu.VMEM((N_HEADS, HEAD_DIM + SUM_ROWS, tq), f32),
                        pltpu.VMEM((S, LANES), bf16),
                        pltpu.VMEM((2, ck, tq), f32),
                        pltpu.VMEM((2, N_HEADS, ck, tq), f32),
                        pltpu.VMEM((2, 2, N_HEADS, tq), f32),
                        pltpu.VMEM((cc, cc), bf16)],
        compiler_params=pltpu.CompilerParams(dimension_semantics=("arbitrary", "arbitrary"),
                                             vmem_limit_bytes=VMEM_LIMIT),
        name="dsa_attn",
    )(qT, qiT, wT, k, vT, ki)


def _conv_kernel(zc_ref, zp_ref, w_ref, b_ref, g_ref, beta_ref, o_ref, ext_ref):
    tm = zc_ref.shape[1]
    i = pl.program_id(1)
    prev = jnp.where(i > 0, zp_ref[0], 0.0)
    ext_ref[0:CONV_HALO, :] = prev
    ext_ref[CONV_HALO:CONV_HALO + tm, :] = zc_ref[0]
    n_ext = CONV_HALO + tm
    ext = ext_ref[...]
    acc = jnp.broadcast_to(b_ref[...], (tm, CONV_CH))
    first = CONV_HALO - (CONV_WIDTH - 1)
    for b in range(SUBLANES):
        offs = [o for o in range(first, first + CONV_WIDTH) if o % SUBLANES == b]
        if not offs:
            continue
        shifted = ext if b == 0 else pltpu.roll(ext, n_ext - b, axis=0)
        for o in offs:
            a = o - b
            acc = acc + w_ref[o - first:o - first + 1, :] * shifted[a:a + tm]
    mu = jnp.mean(acc, axis=-1, keepdims=True)
    d = acc - mu
    var = jnp.mean(d * d, axis=-1, keepdims=True)
    y = d * lax.rsqrt(var + EPS) * g_ref[...] + beta_ref[...]
    o_ref[0] = (y * jax.nn.sigmoid(y)).astype(bf16)


def _conv(z, w_dw, b_dw, ln_g, ln_b, *, tm):
    B, S, C = z.shape
    r = tm // CONV_HALO
    vec = pl.BlockSpec((1, C), lambda b, i: (0, 0))
    return pl.pallas_call(
        _conv_kernel,
        out_shape=jax.ShapeDtypeStruct((B, S, C), bf16),
        grid=(B, S // tm),
        in_specs=[pl.BlockSpec((1, tm, C), lambda b, i: (b, i, 0)),
                  pl.BlockSpec((1, CONV_HALO, C), lambda b, i: (b, jnp.maximum(i * r - 1, 0), 0)),
                  pl.BlockSpec((CONV_WIDTH, C), lambda b, i: (0, 0)),
                  vec, vec, vec],
        out_specs=pl.BlockSpec((1, tm, C), lambda b, i: (b, i, 0)),
        scratch_shapes=[pltpu.VMEM((CONV_HALO + tm, C), f32)],
        compiler_params=pltpu.CompilerParams(dimension_semantics=("arbitrary", "arbitrary")),
        name="conv",
    )(z, z, w_dw.reshape(CONV_WIDTH, C), b_dw.reshape(1, C), ln_g.reshape(1, C), ln_b.reshape(1, C))


def _post_kernel(x_ref, attn_ref, conv_ref, sga_ref, sgb_ref, gm_ref, shf_ref, scf_ref, gf_ref,
                 gffn_ref, gfin_ref, wap_ref, wcp_ref, wout_ref, wfa_ref, wfb_ref, wfo_ref, o_ref,
                 *, final_norm):
    dot = lambda a, w_ref: jnp.dot(a, w_ref[...], preferred_element_type=f32)
    ya = dot(attn_ref[0], wap_ref)
    yc = dot(conv_ref[0], wcp_ref)
    merged = sga_ref[0].astype(f32) * ya + sgb_ref[0].astype(f32) * yc
    x1 = x_ref[0] + gm_ref[0] * dot(merged.astype(bf16), wout_ref)

    ms = jnp.mean(x1 * x1, axis=-1, keepdims=True)
    h = x1 * lax.rsqrt(ms + EPS) * gffn_ref[...]
    hb = (h * (1.0 + scf_ref[0]) + shf_ref[0]).astype(bf16)
    a = dot(hb, wfa_ref)
    b = dot(hb, wfb_ref)
    act = (a * jax.nn.sigmoid(a) * b).astype(bf16)
    x2 = x1 + gf_ref[0] * dot(act, wfo_ref)

    if final_norm:
        ms2 = jnp.mean(x2 * x2, axis=-1, keepdims=True)
        x2 = x2 * lax.rsqrt(ms2 + EPS) * gfin_ref[...]
    o_ref[0] = x2


def _post(x, attn, conv, sga, sgb, g_m, sh_f, sc_f, g_f, norm_ffn_g, norm_final_g,
          w_attn_proj, w_conv_proj, w_out, w_ffn_in, w_ffn_out, *, tm, final_norm):
    B, S, D = x.shape
    d_ff = w_ffn_out.shape[0]
    weights = [w_attn_proj.astype(bf16), w_conv_proj.astype(bf16), w_out.astype(bf16),
               w_ffn_in[:, :d_ff].astype(bf16), w_ffn_in[:, d_ff:].astype(bf16), w_ffn_out.astype(bf16)]
    tile = lambda n: pl.BlockSpec((1, tm, n), lambda b, i: (b, i, 0))
    vec = pl.BlockSpec((1, 1, D), lambda b, i: (b, 0, 0))
    gain = pl.BlockSpec((1, D), lambda b, i: (0, 0))
    in_specs = [tile(D), tile(ATTN_WIDTH), tile(CONV_CH), tile(D), tile(D), vec, vec, vec, vec, gain, gain]
    in_specs += [_resident(w.shape, lambda b, i: (0, 0)) for w in weights]
    return pl.pallas_call(
        functools.partial(_post_kernel, final_norm=final_norm),
        out_shape=jax.ShapeDtypeStruct((B, S, D), f32),
        grid=(B, S // tm),
        in_specs=in_specs,
        out_specs=tile(D),
        compiler_params=pltpu.CompilerParams(dimension_semantics=("arbitrary", "arbitrary"),
                                             vmem_limit_bytes=VMEM_LIMIT),
        name="post",
    )(x, attn, conv, sga, sgb, g_m, sh_f, sc_f, g_f, norm_ffn_g.reshape(1, D),
      norm_final_g.reshape(1, D), *weights)


def kernel(x, c, norm_mix_g, w_in, w_dw, b_dw, conv_ln_g, conv_ln_b, w_attn_proj, w_conv_proj,
           w_out, norm_ffn_g, w_ffn_in, w_ffn_out, w_ada, b_ada, norm_final_g):
    B, S, D = x.shape
    depth = w_in.shape[0]
    topk = min(TOPK_MAX, S // 4)
    c = c.astype(f32)
    for l in range(depth):
        mod = _ada(c, w_ada[l], b_ada[l])
        sh_m, sc_m, g_m, sh_f, sc_f, g_f = [m.reshape(B, 1, D) for m in jnp.split(mod, N_MOD, axis=-1)]
        qT, qiT, vT, k, ki, wT, z, sga, sgb = _inproj(x, sh_m, sc_m, norm_mix_g[l], w_in[l],
                                                      tm=512, ck=ATT_CK)
        attn = _attention(qT, qiT, wT, k, vT, ki, topk=topk)
        conv = _conv(z, w_dw[l], b_dw[l], conv_ln_g[l], conv_ln_b[l], tm=512)
        x = _post(x, attn, conv, sga, sgb, g_m, sh_f, sc_f, g_f, norm_ffn_g[l], norm_final_g,
                  w_attn_proj[l], w_conv_proj[l], w_out[l], w_ffn_in[l], w_ffn_out[l], tm=256,
                  final_norm=(l == depth - 1))
    return x
```

```python
import functools
import math

import jax
import jax.numpy as jnp
import numpy as np
from jax import lax
from jax.experimental import pallas as pl
from jax.experimental.pallas import tpu as pltpu

N_HEADS = 8
HEAD_DIM = 64
ATTN_WIDTH = N_HEADS * HEAD_DIM
IDX_HEADS = 8
IDX_DIM = 64
TOPK_MAX = 256
CONV_CH = 512
CONV_WIDTH = 31
N_MOD = 6
EPS = 1e-6

LANES = 128
SUBLANES = 8
PACKED_ROWS = 16
CONV_HALO = 32
NEG_BIG = -1e30
VMEM_LIMIT = 56 * 1024 * 1024

ATT_TQ = 256
ATT_CK = 512
ATT_CC = 1024
SUM_ROWS = 16
FINE_CHECKS = (9, 11, 13, 15)
LOG2E = math.log2(math.e)

f32 = jnp.float32
bf16 = jnp.bfloat16
coarse_t = jnp.bfloat16


def _resident(block_shape, index_map):
    return pl.BlockSpec(block_shape, index_map, pipeline_mode=pl.Buffered(1))


REDUCE_CHAINS = 2


def _reduce_rows(op, x, rows):
    groups = [x[r * rows:(r + 1) * rows] for r in range(x.shape[0] // rows)]
    chains = groups[:REDUCE_CHAINS]
    for r, g in enumerate(groups[REDUCE_CHAINS:]):
        chains[r % REDUCE_CHAINS] = op(chains[r % REDUCE_CHAINS], g)
    while len(chains) > 1:
        chains = [op(chains[a], chains[a + 1]) for a in range(0, len(chains), 2)]
    return chains[0]


def _ada_kernel(c_ref, w_ref, b_ref, o_ref):
    c = c_ref[...]
    ca = c * jax.nn.sigmoid(c)
    o_ref[...] = jnp.dot(ca, w_ref[...], preferred_element_type=f32,
                         precision=lax.Precision.HIGHEST) + b_ref[...]


def _ada(c, w_ada, b_ada):
    B, D = c.shape
    N = w_ada.shape[1]
    rows = SUBLANES
    c_pad = jnp.zeros((rows, D), f32).at[:B].set(c)
    tn = 1024
    out = pl.pallas_call(
        _ada_kernel,
        out_shape=jax.ShapeDtypeStruct((rows, N), f32),
        grid=(N // tn,),
        in_specs=[pl.BlockSpec((rows, D), lambda j: (0, 0)),
                  pl.BlockSpec((D, tn), lambda j: (0, j)),
                  pl.BlockSpec((1, tn), lambda j: (0, j))],
        out_specs=pl.BlockSpec((rows, tn), lambda j: (0, j)),
        compiler_params=pltpu.CompilerParams(dimension_semantics=("arbitrary",)),
        name="ada",
    )(c_pad, w_ada, b_ada.reshape(1, N))
    return out[:B]


def _inproj_kernel(x_ref, sh_ref, sc_ref, g_ref, wt_ref, wk_ref, wki_ref, wwi_ref,
                   wua_ref, wug_ref, wga_ref, wgb_ref,
                   qT_ref, qiT_ref, vT_ref, k_ref, ki_ref, wT_ref, z_ref, sga_ref, sgb_ref,
                   *, ck):
    x = x_ref[0]
    ms = jnp.mean(x * x, axis=-1, keepdims=True)
    h = x * lax.rsqrt(ms + EPS) * g_ref[...]
    h = h * (1.0 + sc_ref[0]) + sh_ref[0]
    hb = h.astype(bf16)

    def proj(w_ref):
        return jnp.dot(hb, w_ref[...], preferred_element_type=f32)

    ptT = proj(wt_ref).T
    qT_ref[0] = ptT[0:ATTN_WIDTH].astype(bf16)
    qiT_ref[0] = ptT[ATTN_WIDTH:2 * ATTN_WIDTH].astype(bf16)
    tm = x.shape[0]
    for t in range(tm // ck):
        vT_ref[0, t] = ptT[2 * ATTN_WIDTH:3 * ATTN_WIDTH, t * ck:(t + 1) * ck].astype(bf16)
    k_ref[0] = proj(wk_ref).astype(bf16)
    ki_ref[0] = proj(wki_ref).astype(bf16)
    wiT = proj(wwi_ref).T
    wT_ref[0] = wiT[0:IDX_HEADS] * (IDX_HEADS ** -0.5)
    z_ref[0] = proj(wua_ref) * jax.nn.sigmoid(proj(wug_ref))
    sga_ref[0] = jax.nn.sigmoid(proj(wga_ref)).astype(bf16)
    sgb_ref[0] = jax.nn.sigmoid(proj(wgb_ref)).astype(bf16)


def _inproj(x, shift, scale, g, w_in, *, tm, ck):
    B, S, D = x.shape
    W = ATTN_WIDTH
    o = 0
    wq = w_in[:, o:o + W]; o += W
    wk = w_in[:, o:o + W]; o += W
    wv = w_in[:, o:o + W]; o += W
    wqi = w_in[:, o:o + IDX_HEADS * IDX_DIM]; o += IDX_HEADS * IDX_DIM
    wki = w_in[:, o:o + IDX_DIM]; o += IDX_DIM
    wwi = w_in[:, o:o + IDX_HEADS]; o += IDX_HEADS
    wua = w_in[:, o:o + CONV_CH]; o += CONV_CH
    wug = w_in[:, o:o + CONV_CH]; o += CONV_CH
    wga = w_in[:, o:o + D]; o += D
    wgb = w_in[:, o:o + D]; o += D
    wt = jnp.concatenate([wq * (HEAD_DIM ** -0.5 * LOG2E), wqi * (IDX_DIM ** -0.5), wv], axis=1).astype(bf16)
    pad = lambda w: jnp.pad(w, ((0, 0), (0, LANES - w.shape[1]))).astype(bf16)
    weights = [wt, wk.astype(bf16), pad(wki), pad(wwi), wua.astype(bf16), wug.astype(bf16),
               wga.astype(bf16), wgb.astype(bf16)]

    tile = lambda n: pl.BlockSpec((1, tm, n), lambda b, i: (b, i, 0))
    tileT = lambda n: pl.BlockSpec((1, n, tm), lambda b, i: (b, 0, i))
    vec = pl.BlockSpec((1, 1, D), lambda b, i: (b, 0, 0))
    out_shape = [
        jax.ShapeDtypeStruct((B, W, S), bf16),
        jax.ShapeDtypeStruct((B, W, S), bf16),
        jax.ShapeDtypeStruct((B, S // ck, W, ck), bf16),
        jax.ShapeDtypeStruct((B, S, W), bf16),
        jax.ShapeDtypeStruct((B, S, LANES), bf16),
        jax.ShapeDtypeStruct((B, IDX_HEADS, S), f32),
        jax.ShapeDtypeStruct((B, S, CONV_CH), f32),
        jax.ShapeDtypeStruct((B, S, D), bf16),
        jax.ShapeDtypeStruct((B, S, D), bf16),
    ]
    out_specs = [tileT(W), tileT(W),
                 pl.BlockSpec((1, tm // ck, W, ck), lambda b, i: (b, i, 0, 0)),
                 tile(W), tile(LANES), tileT(IDX_HEADS), tile(CONV_CH), tile(D), tile(D)]
    in_specs = [tile(D), vec, vec, pl.BlockSpec((1, D), lambda b, i: (0, 0))]
    in_specs += [_resident(w.shape, lambda b, i: (0, 0)) for w in weights]
    return pl.pallas_call(
        functools.partial(_inproj_kernel, ck=ck),
        out_shape=out_shape,
        grid=(B, S // tm),
        in_specs=in_specs,
        out_specs=out_specs,
        compiler_params=pltpu.CompilerParams(dimension_semantics=("arbitrary", "arbitrary"),
                                             vmem_limit_bytes=VMEM_LIMIT),
        name="inproj",
    )(x, shift, scale, g.reshape(1, D), *weights)


def _ordered_bits_to_float(u):
    key = u ^ jnp.int32(-2 ** 31)
    bits = key ^ ((key >> 31) & jnp.int32(0x7FFFFFFF))
    return lax.bitcast_convert_type(bits, f32)


SLOPE_PIECES = 4


def _slope_pieces(h):
    rem = LOG2E * 2.0 ** (-8.0 * (h + 1) / N_HEADS)
    pieces = []
    for _ in range(SLOPE_PIECES):
        piece = float(np.asarray(rem, np.float32).astype(jnp.bfloat16).astype(np.float32))
        pieces.append(piece)
        rem -= piece
    return pieces


def _attn_kernel(qT_ref, qiT_ref, wT_ref, k_ref, vT_ref, ki_ref, o_ref,
                 sc_ref, sb_ref, qpad_ref, qipad_ref, m_ref, acc_ref, pos_ref, mb_ref, s_ref, stage_ref,
                 tri_ref, *, topk):
    tq = qT_ref.shape[2]
    ck = vT_ref.shape[3]
    cc = ATT_CC
    i = pl.program_id(1)
    q0 = i * tq
    n_keys = q0 + tq
    n_ck = (n_keys + ck - 1) // ck
    n_cc = (n_keys + cc - 1) // cc

    lane1 = lax.broadcasted_iota(jnp.int32, (1, tq), 1)

    @pl.when(i == 0)
    def _():
        for c in range(pos_ref.shape[0] // ck):
            kpos = c * ck + lax.broadcasted_iota(jnp.int32, (ck, LANES), 0)
            lane = lax.broadcasted_iota(jnp.int32, (ck, LANES), 1)
            feat = jnp.where(lane < SLOPE_PIECES, kpos >> 7,
                             jnp.where(lane < 2 * SLOPE_PIECES, kpos & (LANES - 1), 0))
            pos_ref[c * ck:(c + 1) * ck, :] = feat.astype(f32).astype(bf16)
        rowi = lax.broadcasted_iota(jnp.int32, (LANES, tq), 0)
        for h in range(N_HEADS):
            coef = jnp.zeros((LANES, tq), f32)
            for p, piece in enumerate(_slope_pieces(h)):
                coef = jnp.where(rowi == p, LANES * piece, coef)
                coef = jnp.where(rowi == SLOPE_PIECES + p, piece, coef)
            qpad_ref[h, 2 * HEAD_DIM:2 * HEAD_DIM + LANES, :] = coef.astype(bf16)
        tri_ref[...] = jnp.where(lax.broadcasted_iota(jnp.int32, (tq, tq), 1)
                                 < lax.broadcasted_iota(jnp.int32, (tq, tq), 0), 1.0, 0.0).astype(bf16)

    qT = qT_ref[0]
    qiT = qiT_ref[0]
    first_half = lax.broadcasted_iota(jnp.int32, (2 * HEAD_DIM, tq), 0) < HEAD_DIM
    zeros_half = jnp.zeros((IDX_DIM, tq), bf16)
    for h in range(N_HEADS):
        pair = qT[(h // 2) * 2 * HEAD_DIM:(h // 2 + 1) * 2 * HEAD_DIM]
        keep = first_half if h % 2 == 0 else jnp.logical_not(first_half)
        qpad_ref[h, 0:2 * HEAD_DIM, :] = jnp.where(keep, pair, jnp.zeros_like(pair))
    for h in range(IDX_HEADS):
        qipad_ref[h] = jnp.concatenate([qiT[h * IDX_DIM:(h + 1) * IDX_DIM], zeros_half], axis=0)

    w_rows = wT_ref[0]
    row_t = lax.broadcasted_iota(jnp.int32, (tq, tq), 0)
    lane_t = lax.broadcasted_iota(jnp.int32, (tq, tq), 1)

    def score_chunk(j, carry):
        for t in range(ck // tq):
            k0 = pl.multiple_of(j * ck + t * tq, tq)
            kit = ki_ref[0, pl.ds(k0, tq), :]
            acc = jnp.zeros((tq, tq), f32)
            for h in range(IDX_HEADS):
                lg = jnp.dot(kit, qipad_ref[h], preferred_element_type=f32)
                acc = acc + w_rows[h:h + 1, :] * jnp.maximum(lg, 0.0)
            causal = (k0 + row_t) <= (q0 + lane_t)
            acc = jnp.where(causal, acc, -jnp.inf)
            sc_ref[pl.ds(k0, tq), :] = acc
            sb_ref[pl.ds(k0, tq), :] = acc.astype(coarse_t)
        return carry

    lax.fori_loop(0, n_ck, score_chunk, 0)

    @pl.when(n_ck * ck < n_cc * cc)
    def _():
        for r0 in range(ck, cc, ck):
            start = pl.multiple_of(n_ck * ck + (r0 - ck), ck)
            sc_ref[pl.ds(start, ck), :] = jnp.full((ck, tq), -jnp.inf, f32)
            sb_ref[pl.ds(start, ck), :] = jnp.full((ck, tq), -jnp.inf, coarse_t)

    kf = jnp.minimum(topk, q0 + lane1 + 1).astype(f32)

    def count(pred):
        def body(j, acc):
            k0 = pl.multiple_of(j * cc, cc)
            hit = jnp.where(pred(sc_ref[pl.ds(k0, cc), :]), 1.0, 0.0)
            return acc + _reduce_rows(jnp.add, hit, SUBLANES)
        acc = lax.fori_loop(0, n_cc, body, jnp.zeros((SUBLANES, tq), f32))
        return acc.sum(axis=0, keepdims=True)

    def count_coarse(thr16):
        one = jnp.ones((), coarse_t)
        zero = jnp.zeros((), coarse_t)

        def body(j, acc):
            k0 = pl.multiple_of(j * cc, cc)
            hit = jnp.where(sb_ref[pl.ds(k0, cc), :] >= thr16, one, zero)
            return acc + _reduce_rows(jnp.add, hit, PACKED_ROWS).astype(f32)
        acc = lax.fori_loop(0, n_cc, body, jnp.zeros((PACKED_ROWS, tq), f32))
        return acc.sum(axis=0, keepdims=True)

    def coarse_value_bits(v):
        return (v ^ jnp.where((v & 0x8000) != 0, 0x8000, 0xFFFF)) << 16

    def bisect_coarse(it, v):
        cand = v | lax.shift_left(jnp.int32(1), 15 - it)
        thr16 = lax.bitcast_convert_type(coarse_value_bits(cand), f32).astype(coarse_t)
        return jnp.where(count_coarse(thr16) >= kf, cand, v)

    v = lax.fori_loop(0, 16, bisect_coarse, jnp.zeros((1, tq), jnp.int32))
    vbits = coarse_value_bits(v)
    u_mid = vbits ^ jnp.where(vbits < 0, jnp.int32(-1), jnp.int32(-2 ** 31))
    u_base = u_mid - 0x8000
    fine_bits = 17

    def fine_body(it, c):
        d, done = c
        cand = d | lax.shift_left(jnp.int32(1), fine_bits - 1 - it)
        thr_c = _ordered_bits_to_float(u_base + cand)
        n = count(lambda s: s >= thr_c)
        d = jnp.where(jnp.logical_and(done < 0.5, n >= kf), cand, d)
        done = jnp.maximum(done, jnp.where(n == kf, 1.0, 0.0))
        return d, done

    def all_done(c):
        return jnp.min(c[1]) > 0.5

    c = lax.fori_loop(0, FINE_CHECKS[0], fine_body,
                      (jnp.zeros((1, tq), jnp.int32), jnp.zeros((1, tq), f32)))
    for lo, hi in zip(FINE_CHECKS, FINE_CHECKS[1:] + (fine_bits,)):
        c = lax.cond(all_done(c), lambda c: c,
                     functools.partial(lax.fori_loop, lo, hi, fine_body), c)
    d = c[0]
    u = u_base + d
    thr = _ordered_bits_to_float(u)
    thr_next = _ordered_bits_to_float(u + 1)

    @pl.when(jnp.logical_not(all_done(c)))
    def _():
        need = kf - count(lambda s: s >= thr_next)

        def drop_chunk(j, before):
            k0 = pl.multiple_of(j * cc, cc)
            s_all = sc_ref[pl.ds(k0, cc), :]
            kept = []
            for t in range(cc // tq):
                s = s_all[t * tq:(t + 1) * tq]
                tie = (s >= thr) & (s < thr_next)
                tie01 = jnp.where(tie, 1.0, 0.0)
                within = jnp.dot(tri_ref[...], tie01.astype(bf16), preferred_element_type=f32)
                kept.append(jnp.where(tie & (within >= need - before), -jnp.inf, s))
                before = before + _reduce_rows(jnp.add, tie01, SUBLANES).sum(axis=0, keepdims=True)
            sc_ref[pl.ds(k0, cc), :] = jnp.concatenate(kept, axis=0)
            return before

        lax.fori_loop(0, n_cc, drop_chunk, jnp.zeros((1, tq), f32))

    m_ref[...] = jnp.full(m_ref.shape, NEG_BIG, f32)
    acc_ref[...] = jnp.zeros(acc_ref.shape, f32)

    ones_rows = jnp.ones((SUM_ROWS, ck), bf16)

    def logits_step(j, buf):
        k0 = pl.multiple_of(j * ck, ck)
        mb_ref[buf] = jnp.where(sc_ref[pl.ds(k0, ck), :] >= thr, 0.0, NEG_BIG)
        pos = pos_ref[pl.ds(k0, ck), :]
        for h in range(N_HEADS):
            kp = k_ref[0, pl.ds(k0, ck), (h // 2) * LANES:(h // 2 + 1) * LANES]
            s = jnp.dot(jnp.concatenate([kp, pos], axis=1), qpad_ref[h],
                        preferred_element_type=f32)
            s = s + mb_ref[buf]
            s_ref[buf, h] = s
            mt = _reduce_rows(jnp.maximum, s, SUBLANES).max(axis=0, keepdims=True)
            m_old = m_ref[h:h + 1, :]
            m_new = jnp.maximum(m_old, mt)
            m_ref[h:h + 1, :] = m_new
            stage_ref[buf, 0, h:h + 1, :] = m_new
            stage_ref[buf, 1, h:h + 1, :] = jnp.exp2(m_old - m_new)

    def probs_step(j, buf):
        for h in range(N_HEADS):
            p = jnp.exp2(s_ref[buf, h] - stage_ref[buf, 0, h:h + 1, :]).astype(bf16)
            vt = jnp.concatenate([vT_ref[0, j, h * HEAD_DIM:(h + 1) * HEAD_DIM, :], ones_rows], axis=0)
            pv = jnp.dot(vt, p, preferred_element_type=f32)
            acc_ref[h] = stage_ref[buf, 1, h:h + 1, :] * acc_ref[h] + pv

    logits_step(0, 0)

    def chunk_pair(t, carry):
        probs_step(2 * t, 0)
        logits_step(2 * t + 1, 1)
        probs_step(2 * t + 1, 1)
        logits_step(2 * t + 2, 0)
        return carry

    n_pairs = (n_ck - 1) // 2
    lax.fori_loop(0, n_pairs, chunk_pair, 0)
    probs_step(2 * n_pairs, 0)

    @pl.when(2 * n_pairs + 1 < n_ck)
    def _():
        logits_step(n_ck - 1, 1)
        probs_step(n_ck - 1, 1)

    outs = []
    for h in range(N_HEADS):
        outs.append(acc_ref[h, 0:HEAD_DIM, :] / acc_ref[h, HEAD_DIM:HEAD_DIM + 1, :])
    o_ref[0] = jnp.concatenate(outs, axis=0).T.astype(bf16)


def _attention(qT, qiT, wT, k, vT, ki, *, topk):
    B, W, S = qT.shape
    tq, ck, cc = ATT_TQ, ATT_CK, ATT_CC
    assert S % cc == 0 and cc == 2 * ck and ck % tq == 0 and vT.shape[3] == ck
    blkT = lambda n: pl.BlockSpec((1, n, tq), lambda b, i: (b, 0, i))
    return pl.pallas_call(
        functools.partial(_attn_kernel, topk=topk),
        out_shape=jax.ShapeDtypeStruct((B, S, W), bf16),
        grid=(B, S // tq),
        in_specs=[blkT(W), blkT(W), blkT(IDX_HEADS),
                  _resident((1, S, W), lambda b, i: (b, 0, 0)),
                  _resident((1, S // ck, W, ck), lambda b, i: (b, 0, 0, 0)),
                  _resident((1, S, LANES), lambda b, i: (b, 0, 0))],
        out_specs=pl.BlockSpec((1, tq, W), lambda b, i: (b, i, 0)),
        scratch_shapes=[pltpu.VMEM((S, tq), f32),
                        pltpu.VMEM((S, tq), coarse_t),
                        pltpu.VMEM((N_HEADS, 2 * HEAD_DIM + LANES, tq), bf16),
                        pltpu.VMEM((IDX_HEADS, 2 * IDX_DIM, tq), bf16),
                        pltpu.VMEM((N_HEADS, tq), f32),
                        pltpu.VMEM((N_HEADS, HEAD_DIM + SUM_ROWS, tq), f32),
                        pltpu.VMEM((S, LANES), bf16),
                        pltpu.VMEM((2, ck, tq), f32),
                        pltpu.VMEM((2, N_HEADS, ck, tq), f32),
                        pltpu.VMEM((2, 2, N_HEADS, tq), f32),
                        pltpu.VMEM((tq, tq), bf16)],
        compiler_params=pltpu.CompilerParams(dimension_semantics=("arbitrary", "arbitrary"),
                                             vmem_limit_bytes=VMEM_LIMIT),
        name="dsa_attn",
    )(qT, qiT, wT, k, vT, ki)


def _conv_kernel(zc_ref, zp_ref, w_ref, b_ref, g_ref, beta_ref, o_ref, ext_ref):
    tm = zc_ref.shape[1]
    i = pl.program_id(1)
    prev = jnp.where(i > 0, zp_ref[0], 0.0)
    ext_ref[0:CONV_HALO, :] = prev
    ext_ref[CONV_HALO:CONV_HALO + tm, :] = zc_ref[0]
    n_ext = CONV_HALO + tm
    ext = ext_ref[...]
    acc = jnp.broadcast_to(b_ref[...], (tm, CONV_CH))
    first = CONV_HALO - (CONV_WIDTH - 1)
    for b in range(SUBLANES):
        offs = [o for o in range(first, first + CONV_WIDTH) if o % SUBLANES == b]
        if not offs:
            continue
        shifted = ext if b == 0 else pltpu.roll(ext, n_ext - b, axis=0)
        for o in offs:
            a = o - b
            acc = acc + w_ref[o - first:o - first + 1, :] * shifted[a:a + tm]
    mu = jnp.mean(acc, axis=-1, keepdims=True)
    d = acc - mu
    var = jnp.mean(d * d, axis=-1, keepdims=True)
    y = d * lax.rsqrt(var + EPS) * g_ref[...] + beta_ref[...]
    o_ref[0] = (y * jax.nn.sigmoid(y)).astype(bf16)


def _conv(z, w_dw, b_dw, ln_g, ln_b, *, tm):
    B, S, C = z.shape
    r = tm // CONV_HALO
    vec = pl.BlockSpec((1, C), lambda b, i: (0, 0))
    return pl.pallas_call(
        _conv_kernel,
        out_shape=jax.ShapeDtypeStruct((B, S, C), bf16),
        grid=(B, S // tm),
        in_specs=[pl.BlockSpec((1, tm, C), lambda b, i: (b, i, 0)),
                  pl.BlockSpec((1, CONV_HALO, C), lambda b, i: (b, jnp.maximum(i * r - 1, 0), 0)),
                  pl.BlockSpec((CONV_WIDTH, C), lambda b, i: (0, 0)),
                  vec, vec, vec],
        out_specs=pl.BlockSpec((1, tm, C), lambda b, i: (b, i, 0)),
        scratch_shapes=[pltpu.VMEM((CONV_HALO + tm, C), f32)],
        compiler_params=pltpu.CompilerParams(dimension_semantics=("arbitrary", "arbitrary")),
        name="conv",
    )(z, z, w_dw.reshape(CONV_WIDTH, C), b_dw.reshape(1, C), ln_g.reshape(1, C), ln_b.reshape(1, C))


def _post_kernel(x_ref, attn_ref, conv_ref, sga_ref, sgb_ref, gm_ref, shf_ref, scf_ref, gf_ref,
                 gffn_ref, gfin_ref, wap_ref, wcp_ref, wout_ref, wfa_ref, wfb_ref, wfo_ref, o_ref,
                 *, final_norm):
    dot = lambda a, w_ref: jnp.dot(a, w_ref[...], preferred_element_type=f32)
    ya = dot(attn_ref[0], wap_ref)
    yc = dot(conv_ref[0], wcp_ref)
    merged = sga_ref[0].astype(f32) * ya + sgb_ref[0].astype(f32) * yc
    x1 = x_ref[0] + gm_ref[0] * dot(merged.astype(bf16), wout_ref)

    ms = jnp.mean(x1 * x1, axis=-1, keepdims=True)
    h = x1 * lax.rsqrt(ms + EPS) * gffn_ref[...]
    hb = (h * (1.0 + scf_ref[0]) + shf_ref[0]).astype(bf16)
    a = dot(hb, wfa_ref)
    b = dot(hb, wfb_ref)
    act = (a * jax.nn.sigmoid(a) * b).astype(bf16)
    x2 = x1 + gf_ref[0] * dot(act, wfo_ref)

    if final_norm:
        ms2 = jnp.mean(x2 * x2, axis=-1, keepdims=True)
        x2 = x2 * lax.rsqrt(ms2 + EPS) * gfin_ref[...]
    o_ref[0] = x2


def _post(x, attn, conv, sga, sgb, g_m, sh_f, sc_f, g_f, norm_ffn_g, norm_final_g,
          w_attn_proj, w_conv_proj, w_out, w_ffn_in, w_ffn_out, *, tm, final_norm):
    B, S, D = x.shape
    d_ff = w_ffn_out.shape[0]
    weights = [w_attn_proj.astype(bf16), w_conv_proj.astype(bf16), w_out.astype(bf16),
               w_ffn_in[:, :d_ff].astype(bf16), w_ffn_in[:, d_ff:].astype(bf16), w_ffn_out.astype(bf16)]
    tile = lambda n: pl.BlockSpec((1, tm, n), lambda b, i: (b, i, 0))
    vec = pl.BlockSpec((1, 1, D), lambda b, i: (b, 0, 0))
    gain = pl.BlockSpec((1, D), lambda b, i: (0, 0))
    in_specs = [tile(D), tile(ATTN_WIDTH), tile(CONV_CH), tile(D), tile(D), vec, vec, vec, vec, gain, gain]
    in_specs += [_resident(w.shape, lambda b, i: (0, 0)) for w in weights]
    return pl.pallas_call(
        functools.partial(_post_kernel, final_norm=final_norm),
        out_shape=jax.ShapeDtypeStruct((B, S, D), f32),
        grid=(B, S // tm),
        in_specs=in_specs,
        out_specs=tile(D),
        compiler_params=pltpu.CompilerParams(dimension_semantics=("arbitrary", "arbitrary"),
                                             vmem_limit_bytes=VMEM_LIMIT),
        name="post",
    )(x, attn, conv, sga, sgb, g_m, sh_f, sc_f, g_f, norm_ffn_g.reshape(1, D),
      norm_final_g.reshape(1, D), *weights)


def kernel(x, c, norm_mix_g, w_in, w_dw, b_dw, conv_ln_g, conv_ln_b, w_attn_proj, w_conv_proj,
           w_out, norm_ffn_g, w_ffn_in, w_ffn_out, w_ada, b_ada, norm_final_g):
    B, S, D = x.shape
    depth = w_in.shape[0]
    topk = min(TOPK_MAX, S // 4)
    c = c.astype(f32)
    for l in range(depth):
        mod = _ada(c, w_ada[l], b_ada[l])
        sh_m, sc_m, g_m, sh_f, sc_f, g_f = [m.reshape(B, 1, D) for m in jnp.split(mod, N_MOD, axis=-1)]
        qT, qiT, vT, k, ki, wT, z, sga, sgb = _inproj(x, sh_m, sc_m, norm_mix_g[l], w_in[l],
                                                      tm=512, ck=ATT_CK)
        attn = _attention(qT, qiT, wT, k, vT, ki, topk=topk)
        conv = _conv(z, w_dw[l], b_dw[l], conv_ln_g[l], conv_ln_b[l], tm=512)
        x = _post(x, attn, conv, sga, sgb, g_m, sh_f, sc_f, g_f, norm_ffn_g[l], norm_final_g,
                  w_attn_proj[l], w_conv_proj[l], w_out[l], w_ffn_in[l], w_ffn_out[l], tm=512,
                  final_norm=(l == depth - 1))
    return x
```

```python
import functools
import math

import jax
import jax.numpy as jnp
import numpy as np
from jax import lax
from jax.experimental import pallas as pl
from jax.experimental.pallas import tpu as pltpu

N_HEADS = 8
HEAD_DIM = 64
ATTN_WIDTH = N_HEADS * HEAD_DIM
IDX_HEADS = 8
IDX_DIM = 64
TOPK_MAX = 256
CONV_CH = 512
CONV_WIDTH = 31
N_MOD = 6
EPS = 1e-6

LANES = 128
SUBLANES = 8
PACKED_ROWS = 16
CONV_HALO = 32
NEG_BIG = -1e30
VMEM_LIMIT = 56 * 1024 * 1024

ATT_TQ = 256
ATT_CK = 512
ATT_CC = 1024
SUM_ROWS = 16
FINE_CHECKS = (9, 11, 13, 15)
LOG2E = math.log2(math.e)

f32 = jnp.float32
bf16 = jnp.bfloat16
coarse_t = jnp.bfloat16


def _resident(block_shape, index_map):
    return pl.BlockSpec(block_shape, index_map, pipeline_mode=pl.Buffered(1))


REDUCE_CHAINS = 2


def _reduce_rows(op, x, rows):
    groups = [x[r * rows:(r + 1) * rows] for r in range(x.shape[0] // rows)]
    chains = groups[:REDUCE_CHAINS]
    for r, g in enumerate(groups[REDUCE_CHAINS:]):
        chains[r % REDUCE_CHAINS] = op(chains[r % REDUCE_CHAINS], g)
    while len(chains) > 1:
        chains = [op(chains[a], chains[a + 1]) for a in range(0, len(chains), 2)]
    return chains[0]


def _ada_kernel(c_ref, w_ref, b_ref, o_ref):
    c = c_ref[...]
    ca = c * jax.nn.sigmoid(c)
    o_ref[...] = jnp.dot(ca, w_ref[...], preferred_element_type=f32,
                         precision=lax.Precision.HIGHEST) + b_ref[...]


def _ada(c, w_ada, b_ada):
    B, D = c.shape
    N = w_ada.shape[1]
    rows = SUBLANES
    c_pad = jnp.zeros((rows, D), f32).at[:B].set(c)
    tn = 1024
    out = pl.pallas_call(
        _ada_kernel,
        out_shape=jax.ShapeDtypeStruct((rows, N), f32),
        grid=(N // tn,),
        in_specs=[pl.BlockSpec((rows, D), lambda j: (0, 0)),
                  pl.BlockSpec((D, tn), lambda j: (0, j)),
                  pl.BlockSpec((1, tn), lambda j: (0, j))],
        out_specs=pl.BlockSpec((rows, tn), lambda j: (0, j)),
        compiler_params=pltpu.CompilerParams(dimension_semantics=("arbitrary",)),
        name="ada",
    )(c_pad, w_ada, b_ada.reshape(1, N))
    return out[:B]


def _inproj_kernel(x_ref, sh_ref, sc_ref, g_ref, wt_ref, wk_ref, wki_ref, wwi_ref,
                   wua_ref, wug_ref, wga_ref, wgb_ref,
                   qT_ref, qiT_ref, vT_ref, k_ref, ki_ref, wT_ref, z_ref, sga_ref, sgb_ref,
                   *, ck):
    x = x_ref[0]
    ms = jnp.mean(x * x, axis=-1, keepdims=True)
    h = x * lax.rsqrt(ms + EPS) * g_ref[...]
    h = h * (1.0 + sc_ref[0]) + sh_ref[0]
    hb = h.astype(bf16)

    def proj(w_ref):
        return jnp.dot(hb, w_ref[...], preferred_element_type=f32)

    ptT = proj(wt_ref).T
    qT_ref[0] = ptT[0:ATTN_WIDTH].astype(bf16)
    qiT_ref[0] = ptT[ATTN_WIDTH:2 * ATTN_WIDTH].astype(bf16)
    tm = x.shape[0]
    for t in range(tm // ck):
        vT_ref[0, t] = ptT[2 * ATTN_WIDTH:3 * ATTN_WIDTH, t * ck:(t + 1) * ck].astype(bf16)
    k_ref[0] = proj(wk_ref).astype(bf16)
    ki_ref[0] = proj(wki_ref).astype(bf16)
    wiT = proj(wwi_ref).T
    wT_ref[0] = wiT[0:IDX_HEADS] * (IDX_HEADS ** -0.5)
    z_ref[0] = proj(wua_ref) * jax.nn.sigmoid(proj(wug_ref))
    sga_ref[0] = jax.nn.sigmoid(proj(wga_ref)).astype(bf16)
    sgb_ref[0] = jax.nn.sigmoid(proj(wgb_ref)).astype(bf16)


def _inproj(x, shift, scale, g, w_in, *, tm, ck):
    B, S, D = x.shape
    W = ATTN_WIDTH
    o = 0
    wq = w_in[:, o:o + W]; o += W
    wk = w_in[:, o:o + W]; o += W
    wv = w_in[:, o:o + W]; o += W
    wqi = w_in[:, o:o + IDX_HEADS * IDX_DIM]; o += IDX_HEADS * IDX_DIM
    wki = w_in[:, o:o + IDX_DIM]; o += IDX_DIM
    wwi = w_in[:, o:o + IDX_HEADS]; o += IDX_HEADS
    wua = w_in[:, o:o + CONV_CH]; o += CONV_CH
    wug = w_in[:, o:o + CONV_CH]; o += CONV_CH
    wga = w_in[:, o:o + D]; o += D
    wgb = w_in[:, o:o + D]; o += D
    wt = jnp.concatenate([wq * (HEAD_DIM ** -0.5 * LOG2E), wqi * (IDX_DIM ** -0.5), wv], axis=1).astype(bf16)
    pad = lambda w: jnp.pad(w, ((0, 0), (0, LANES - w.shape[1]))).astype(bf16)
    weights = [wt, wk.astype(bf16), pad(wki), pad(wwi), wua.astype(bf16), wug.astype(bf16),
               wga.astype(bf16), wgb.astype(bf16)]

    tile = lambda n: pl.BlockSpec((1, tm, n), lambda b, i: (b, i, 0))
    tileT = lambda n: pl.BlockSpec((1, n, tm), lambda b, i: (b, 0, i))
    vec = pl.BlockSpec((1, 1, D), lambda b, i: (b, 0, 0))
    out_shape = [
        jax.ShapeDtypeStruct((B, W, S), bf16),
        jax.ShapeDtypeStruct((B, W, S), bf16),
        jax.ShapeDtypeStruct((B, S // ck, W, ck), bf16),
        jax.ShapeDtypeStruct((B, S, W), bf16),
        jax.ShapeDtypeStruct((B, S, LANES), bf16),
        jax.ShapeDtypeStruct((B, IDX_HEADS, S), f32),
        jax.ShapeDtypeStruct((B, S, CONV_CH), f32),
        jax.ShapeDtypeStruct((B, S, D), bf16),
        jax.ShapeDtypeStruct((B, S, D), bf16),
    ]
    out_specs = [tileT(W), tileT(W),
                 pl.BlockSpec((1, tm // ck, W, ck), lambda b, i: (b, i, 0, 0)),
                 tile(W), tile(LANES), tileT(IDX_HEADS), tile(CONV_CH), tile(D), tile(D)]
    in_specs = [tile(D), vec, vec, pl.BlockSpec((1, D), lambda b, i: (0, 0))]
    in_specs += [_resident(w.shape, lambda b, i: (0, 0)) for w in weights]
    return pl.pallas_call(
        functools.partial(_inproj_kernel, ck=ck),
        out_shape=out_shape,
        grid=(B, S // tm),
        in_specs=in_specs,
        out_specs=out_specs,
        compiler_params=pltpu.CompilerParams(dimension_semantics=("arbitrary", "arbitrary"),
                                             vmem_limit_bytes=VMEM_LIMIT),
        name="inproj",
    )(x, shift, scale, g.reshape(1, D), *weights)


def _ordered_bits_to_float(u):
    key = u ^ jnp.int32(-2 ** 31)
    bits = key ^ ((key >> 31) & jnp.int32(0x7FFFFFFF))
    return lax.bitcast_convert_type(bits, f32)


SLOPE_PIECES = 4


def _slope_pieces(h):
    rem = LOG2E * 2.0 ** (-8.0 * (h + 1) / N_HEADS)
    pieces = []
    for _ in range(SLOPE_PIECES):
        piece = float(np.asarray(rem, np.float32).astype(jnp.bfloat16).astype(np.float32))
        pieces.append(piece)
        rem -= piece
    return pieces


def _attn_kernel(qT_ref, qiT_ref, wT_ref, k_ref, vT_ref, ki_ref, o_ref,
                 sc_ref, sb_ref, qpad_ref, qipad_ref, m_ref, acc_ref, pos_ref, mb_ref, s_ref, stage_ref,
                 tri_ref, *, topk):
    tq = qT_ref.shape[2]
    ck = vT_ref.shape[3]
    cc = ATT_CC
    i = pl.program_id(1)
    q0 = i * tq
    n_keys = q0 + tq
    n_ck = (n_keys + ck - 1) // ck
    n_cc = (n_keys + cc - 1) // cc

    lane1 = lax.broadcasted_iota(jnp.int32, (1, tq), 1)

    @pl.when(i == 0)
    def _():
        for c in range(pos_ref.shape[0] // ck):
            kpos = c * ck + lax.broadcasted_iota(jnp.int32, (ck, LANES), 0)
            lane = lax.broadcasted_iota(jnp.int32, (ck, LANES), 1)
            feat = jnp.where(lane < SLOPE_PIECES, kpos >> 7,
                             jnp.where(lane < 2 * SLOPE_PIECES, kpos & (LANES - 1), 0))
            pos_ref[c * ck:(c + 1) * ck, :] = feat.astype(f32).astype(bf16)
        rowi = lax.broadcasted_iota(jnp.int32, (LANES, tq), 0)
        for h in range(N_HEADS):
            coef = jnp.zeros((LANES, tq), f32)
            for p, piece in enumerate(_slope_pieces(h)):
                coef = jnp.where(rowi == p, LANES * piece, coef)
                coef = jnp.where(rowi == SLOPE_PIECES + p, piece, coef)
            qpad_ref[h, 2 * HEAD_DIM:2 * HEAD_DIM + LANES, :] = coef.astype(bf16)
        tri_ref[...] = jnp.where(lax.broadcasted_iota(jnp.int32, (tq, tq), 1)
                                 < lax.broadcasted_iota(jnp.int32, (tq, tq), 0), 1.0, 0.0).astype(bf16)

    qT = qT_ref[0]
    qiT = qiT_ref[0]
    first_half = lax.broadcasted_iota(jnp.int32, (2 * HEAD_DIM, tq), 0) < HEAD_DIM
    zeros_half = jnp.zeros((IDX_DIM, tq), bf16)
    for h in range(N_HEADS):
        pair = qT[(h // 2) * 2 * HEAD_DIM:(h // 2 + 1) * 2 * HEAD_DIM]
        keep = first_half if h % 2 == 0 else jnp.logical_not(first_half)
        qpad_ref[h, 0:2 * HEAD_DIM, :] = jnp.where(keep, pair, jnp.zeros_like(pair))
    for h in range(IDX_HEADS):
        qipad_ref[h] = jnp.concatenate([qiT[h * IDX_DIM:(h + 1) * IDX_DIM], zeros_half], axis=0)

    w_rows = wT_ref[0]
    row_t = lax.broadcasted_iota(jnp.int32, (tq, tq), 0)
    lane_t = lax.broadcasted_iota(jnp.int32, (tq, tq), 1)

    def score_chunk(j, carry):
        for t in range(ck // tq):
            k0 = pl.multiple_of(j * ck + t * tq, tq)
            kit = ki_ref[0, pl.ds(k0, tq), :]
            acc = jnp.zeros((tq, tq), f32)
            for h in range(IDX_HEADS):
                lg = jnp.dot(kit, qipad_ref[h], preferred_element_type=f32)
                acc = acc + w_rows[h:h + 1, :] * jnp.maximum(lg, 0.0)
            causal = (k0 + row_t) <= (q0 + lane_t)
            acc = jnp.where(causal, acc, -jnp.inf)
            sc_ref[pl.ds(k0, tq), :] = acc
            sb_ref[pl.ds(k0, tq), :] = acc.astype(coarse_t)
        return carry

    lax.fori_loop(0, n_ck, score_chunk, 0)

    @pl.when(n_ck * ck < n_cc * cc)
    def _():
        for r0 in range(ck, cc, ck):
            start = pl.multiple_of(n_ck * ck + (r0 - ck), ck)
            sc_ref[pl.ds(start, ck), :] = jnp.full((ck, tq), -jnp.inf, f32)
            sb_ref[pl.ds(start, ck), :] = jnp.full((ck, tq), -jnp.inf, coarse_t)

    kf = jnp.minimum(topk, q0 + lane1 + 1).astype(f32)

    def count(pred):
        def body(j, acc):
            k0 = pl.multiple_of(j * cc, cc)
            hit = jnp.where(pred(sc_ref[pl.ds(k0, cc), :]), 1.0, 0.0)
            return acc + _reduce_rows(jnp.add, hit, SUBLANES)
        acc = lax.fori_loop(0, n_cc, body, jnp.zeros((SUBLANES, tq), f32))
        return acc.sum(axis=0, keepdims=True)

    def count_coarse(thr16):
        one = jnp.ones((), coarse_t)
        zero = jnp.zeros((), coarse_t)

        def body(j, acc):
            k0 = pl.multiple_of(j * cc, cc)
            hit = jnp.where(sb_ref[pl.ds(k0, cc), :] >= thr16, one, zero)
            return acc + _reduce_rows(jnp.add, hit, PACKED_ROWS).astype(f32)
        acc = lax.fori_loop(0, n_cc, body, jnp.zeros((PACKED_ROWS, tq), f32))
        return acc.sum(axis=0, keepdims=True)

    def coarse_value_bits(v):
        return (v ^ jnp.where((v & 0x8000) != 0, 0x8000, 0xFFFF)) << 16

    def bisect_coarse(it, v):
        cand = v | lax.shift_left(jnp.int32(1), 15 - it)
        thr16 = lax.bitcast_convert_type(coarse_value_bits(cand), f32).astype(coarse_t)
        return jnp.where(count_coarse(thr16) >= kf, cand, v)

    v = lax.fori_loop(0, 16, bisect_coarse, jnp.zeros((1, tq), jnp.int32))
    vbits = coarse_value_bits(v)
    u_mid = vbits ^ jnp.where(vbits < 0, jnp.int32(-1), jnp.int32(-2 ** 31))
    u_base = u_mid - 0x8000
    fine_bits = 17

    def fine_body(it, c):
        d, done = c
        cand = d | lax.shift_left(jnp.int32(1), fine_bits - 1 - it)
        thr_c = _ordered_bits_to_float(u_base + cand)
        n = count(lambda s: s >= thr_c)
        d = jnp.where(jnp.logical_and(done < 0.5, n >= kf), cand, d)
        done = jnp.maximum(done, jnp.where(n == kf, 1.0, 0.0))
        return d, done

    def all_done(c):
        return jnp.min(c[1]) > 0.5

    c = lax.fori_loop(0, FINE_CHECKS[0], fine_body,
                      (jnp.zeros((1, tq), jnp.int32), jnp.zeros((1, tq), f32)))
    for lo, hi in zip(FINE_CHECKS, FINE_CHECKS[1:] + (fine_bits,)):
        c = lax.cond(all_done(c), lambda c: c,
                     functools.partial(lax.fori_loop, lo, hi, fine_body), c)
    d = c[0]
    u = u_base + d
    thr = _ordered_bits_to_float(u)
    thr_next = _ordered_bits_to_float(u + 1)

    @pl.when(jnp.logical_not(all_done(c)))
    def _():
        need = kf - count(lambda s: s >= thr_next)

        def drop_chunk(j, before):
            k0 = pl.multiple_of(j * cc, cc)
            s_all = sc_ref[pl.ds(k0, cc), :]
            kept = []
            for t in range(cc // tq):
                s = s_all[t * tq:(t + 1) * tq]
                tie = (s >= thr) & (s < thr_next)
                tie01 = jnp.where(tie, 1.0, 0.0)
                within = jnp.dot(tri_ref[...], tie01.astype(bf16), preferred_element_type=f32)
                kept.append(jnp.where(tie & (within >= need - before), -jnp.inf, s))
                before = before + _reduce_rows(jnp.add, tie01, SUBLANES).sum(axis=0, keepdims=True)
            sc_ref[pl.ds(k0, cc), :] = jnp.concatenate(kept, axis=0)
            return before

        lax.fori_loop(0, n_cc, drop_chunk, jnp.zeros((1, tq), f32))

    m_ref[...] = jnp.full(m_ref.shape, NEG_BIG, f32)
    acc_ref[...] = jnp.zeros(acc_ref.shape, f32)

    ones_rows = jnp.ones((SUM_ROWS, ck), bf16)

    def logits_head(j, buf, h):
        k0 = pl.multiple_of(j * ck, ck)
        kp = k_ref[0, pl.ds(k0, ck), (h // 2) * LANES:(h // 2 + 1) * LANES]
        s = jnp.dot(jnp.concatenate([kp, pos_ref[pl.ds(k0, ck), :]], axis=1), qpad_ref[h],
                    preferred_element_type=f32)
        s = s + mb_ref[buf]
        s_ref[buf, h] = s
        mt = _reduce_rows(jnp.maximum, s, SUBLANES).max(axis=0, keepdims=True)
        m_old = m_ref[h:h + 1, :]
        m_new = jnp.maximum(m_old, mt)
        m_ref[h:h + 1, :] = m_new
        stage_ref[buf, 0, h:h + 1, :] = m_new
        stage_ref[buf, 1, h:h + 1, :] = jnp.exp2(m_old - m_new)

    def probs_head(j, buf, h):
        p = jnp.exp2(s_ref[buf, h] - stage_ref[buf, 0, h:h + 1, :]).astype(bf16)
        vt = jnp.concatenate([vT_ref[0, j, h * HEAD_DIM:(h + 1) * HEAD_DIM, :], ones_rows], axis=0)
        pv = jnp.dot(vt, p, preferred_element_type=f32)
        acc_ref[h] = stage_ref[buf, 1, h:h + 1, :] * acc_ref[h] + pv

    def mask_step(j, buf):
        k0 = pl.multiple_of(j * ck, ck)
        mb_ref[buf] = jnp.where(sc_ref[pl.ds(k0, ck), :] >= thr, 0.0, NEG_BIG)

    def logits_step(j, buf):
        mask_step(j, buf)
        for h in range(N_HEADS):
            logits_head(j, buf, h)

    def probs_step(j, buf):
        for h in range(N_HEADS):
            probs_head(j, buf, h)

    def overlapped(jp, bufp, jl, bufl):
        mask_step(jl, bufl)
        for h in range(N_HEADS):
            logits_head(jl, bufl, h)
            probs_head(jp, bufp, h)

    logits_step(0, 0)

    def chunk_pair(t, carry):
        overlapped(2 * t, 0, 2 * t + 1, 1)
        overlapped(2 * t + 1, 1, 2 * t + 2, 0)
        return carry

    n_pairs = (n_ck - 1) // 2
    lax.fori_loop(0, n_pairs, chunk_pair, 0)
    probs_step(2 * n_pairs, 0)

    @pl.when(2 * n_pairs + 1 < n_ck)
    def _():
        logits_step(n_ck - 1, 1)
        probs_step(n_ck - 1, 1)

    outs = []
    for h in range(N_HEADS):
        outs.append(acc_ref[h, 0:HEAD_DIM, :] / acc_ref[h, HEAD_DIM:HEAD_DIM + 1, :])
    o_ref[0] = jnp.concatenate(outs, axis=0).T.astype(bf16)


def _attention(qT, qiT, wT, k, vT, ki, *, topk):
    B, W, S = qT.shape
    tq, ck, cc = ATT_TQ, ATT_CK, ATT_CC
    assert S % cc == 0 and cc == 2 * ck and ck % tq == 0 and vT.shape[3] == ck
    blkT = lambda n: pl.BlockSpec((1, n, tq), lambda b, i: (b, 0, i))
    return pl.pallas_call(
        functools.partial(_attn_kernel, topk=topk),
        out_shape=jax.ShapeDtypeStruct((B, S, W), bf16),
        grid=(B, S // tq),
        in_specs=[blkT(W), blkT(W), blkT(IDX_HEADS),
                  _resident((1, S, W), lambda b, i: (b, 0, 0)),
                  _resident((1, S // ck, W, ck), lambda b, i: (b, 0, 0, 0)),
                  _resident((1, S, LANES), lambda b, i: (b, 0, 0))],
        out_specs=pl.BlockSpec((1, tq, W), lambda b, i: (b, i, 0)),
        scratch_shapes=[pltpu.VMEM((S, tq), f32),
                        pltpu.VMEM((S, tq), coarse_t),
                        pltpu.VMEM((N_HEADS, 2 * HEAD_DIM + LANES, tq), bf16),
                        pltpu.VMEM((IDX_HEADS, 2 * IDX_DIM, tq), bf16),
                        pltpu.VMEM((N_HEADS, tq), f32),
                        pltpu.VMEM((N_HEADS, HEAD_DIM + SUM_ROWS, tq), f32),
                        pltpu.VMEM((S, LANES), bf16),
                        pltpu.VMEM((2, ck, tq), f32),
                        pltpu.VMEM((2, N_HEADS, ck, tq), f32),
                        pltpu.VMEM((2, 2, N_HEADS, tq), f32),
                        pltpu.VMEM((tq, tq), bf16)],
        compiler_params=pltpu.CompilerParams(dimension_semantics=("arbitrary", "arbitrary"),
                                             vmem_limit_bytes=VMEM_LIMIT),
        name="dsa_attn",
    )(qT, qiT, wT, k, vT, ki)


def _conv_kernel(zc_ref, zp_ref, w_ref, b_ref, g_ref, beta_ref, o_ref, ext_ref):
    tm = zc_ref.shape[1]
    i = pl.program_id(1)
    prev = jnp.where(i > 0, zp_ref[0], 0.0)
    ext_ref[0:CONV_HALO, :] = prev
    ext_ref[CONV_HALO:CONV_HALO + tm, :] = zc_ref[0]
    n_ext = CONV_HALO + tm
    ext = ext_ref[...]
    acc = jnp.broadcast_to(b_ref[...], (tm, CONV_CH))
    first = CONV_HALO - (CONV_WIDTH - 1)
    for b in range(SUBLANES):
        offs = [o for o in range(first, first + CONV_WIDTH) if o % SUBLANES == b]
        if not offs:
            continue
        shifted = ext if b == 0 else pltpu.roll(ext, n_ext - b, axis=0)
        for o in offs:
            a = o - b
            acc = acc + w_ref[o - first:o - first + 1, :] * shifted[a:a + tm]
    mu = jnp.mean(acc, axis=-1, keepdims=True)
    d = acc - mu
    var = jnp.mean(d * d, axis=-1, keepdims=True)
    y = d * lax.rsqrt(var + EPS) * g_ref[...] + beta_ref[...]
    o_ref[0] = (y * jax.nn.sigmoid(y)).astype(bf16)


def _conv(z, w_dw, b_dw, ln_g, ln_b, *, tm):
    B, S, C = z.shape
    r = tm // CONV_HALO
    vec = pl.BlockSpec((1, C), lambda b, i: (0, 0))
    return pl.pallas_call(
        _conv_kernel,
        out_shape=jax.ShapeDtypeStruct((B, S, C), bf16),
        grid=(B, S // tm),
        in_specs=[pl.BlockSpec((1, tm, C), lambda b, i: (b, i, 0)),
                  pl.BlockSpec((1, CONV_HALO, C), lambda b, i: (b, jnp.maximum(i * r - 1, 0), 0)),
                  pl.BlockSpec((CONV_WIDTH, C), lambda b, i: (0, 0)),
                  vec, vec, vec],
        out_specs=pl.BlockSpec((1, tm, C), lambda b, i: (b, i, 0)),
        scratch_shapes=[pltpu.VMEM((CONV_HALO + tm, C), f32)],
        compiler_params=pltpu.CompilerParams(dimension_semantics=("arbitrary", "arbitrary")),
        name="conv",
    )(z, z, w_dw.reshape(CONV_WIDTH, C), b_dw.reshape(1, C), ln_g.reshape(1, C), ln_b.reshape(1, C))


def _post_kernel(x_ref, attn_ref, conv_ref, sga_ref, sgb_ref, gm_ref, shf_ref, scf_ref, gf_ref,
                 gffn_ref, gfin_ref, wap_ref, wcp_ref, wout_ref, wfa_ref, wfb_ref, wfo_ref, o_ref,
                 *, final_norm):
    dot = lambda a, w_ref: jnp.dot(a, w_ref[...], preferred_element_type=f32)
    ya = dot(attn_ref[0], wap_ref)
    yc = dot(conv_ref[0], wcp_ref)
    merged = sga_ref[0].astype(f32) * ya + sgb_ref[0].astype(f32) * yc
    x1 = x_ref[0] + gm_ref[0] * dot(merged.astype(bf16), wout_ref)

    ms = jnp.mean(x1 * x1, axis=-1, keepdims=True)
    h = x1 * lax.rsqrt(ms + EPS) * gffn_ref[...]
    hb = (h * (1.0 + scf_ref[0]) + shf_ref[0]).astype(bf16)
    a = dot(hb, wfa_ref)
    b = dot(hb, wfb_ref)
    act = (a * jax.nn.sigmoid(a) * b).astype(bf16)
    x2 = x1 + gf_ref[0] * dot(act, wfo_ref)

    if final_norm:
        ms2 = jnp.mean(x2 * x2, axis=-1, keepdims=True)
        x2 = x2 * lax.rsqrt(ms2 + EPS) * gfin_ref[...]
    o_ref[0] = x2


def _post(x, attn, conv, sga, sgb, g_m, sh_f, sc_f, g_f, norm_ffn_g, norm_final_g,
          w_attn_proj, w_conv_proj, w_out, w_ffn_in, w_ffn_out, *, tm, final_norm):
    B, S, D = x.shape
    d_ff = w_ffn_out.shape[0]
    weights = [w_attn_proj.astype(bf16), w_conv_proj.astype(bf16), w_out.astype(bf16),
               w_ffn_in[:, :d_ff].astype(bf16), w_ffn_in[:, d_ff:].astype(bf16), w_ffn_out.astype(bf16)]
    tile = lambda n: pl.BlockSpec((1, tm, n), lambda b, i: (b, i, 0))
    vec = pl.BlockSpec((1, 1, D), lambda b, i: (b, 0, 0))
    gain = pl.BlockSpec((1, D), lambda b, i: (0, 0))
    in_specs = [tile(D), tile(ATTN_WIDTH), tile(CONV_CH), tile(D), tile(D), vec, vec, vec, vec, gain, gain]
    in_specs += [_resident(w.shape, lambda b, i: (0, 0)) for w in weights]
    return pl.pallas_call(
        functools.partial(_post_kernel, final_norm=final_norm),
        out_shape=jax.ShapeDtypeStruct((B, S, D), f32),
        grid=(B, S // tm),
        in_specs=in_specs,
        out_specs=tile(D),
        compiler_params=pltpu.CompilerParams(dimension_semantics=("arbitrary", "arbitrary"),
                                             vmem_limit_bytes=VMEM_LIMIT),
        name="post",
    )(x, attn, conv, sga, sgb, g_m, sh_f, sc_f, g_f, norm_ffn_g.reshape(1, D),
      norm_final_g.reshape(1, D), *weights)


def kernel(x, c, norm_mix_g, w_in, w_dw, b_dw, conv_ln_g, conv_ln_b, w_attn_proj, w_conv_proj,
           w_out, norm_ffn_g, w_ffn_in, w_ffn_out, w_ada, b_ada, norm_final_g):
    B, S, D = x.shape
    depth = w_in.shape[0]
    topk = min(TOPK_MAX, S // 4)
    c = c.astype(f32)
    for l in range(depth):
        mod = _ada(c, w_ada[l], b_ada[l])
        sh_m, sc_m, g_m, sh_f, sc_f, g_f = [m.reshape(B, 1, D) for m in jnp.split(mod, N_MOD, axis=-1)]
        qT, qiT, vT, k, ki, wT, z, sga, sgb = _inproj(x, sh_m, sc_m, norm_mix_g[l], w_in[l],
                                                      tm=512, ck=ATT_CK)
        attn = _attention(qT, qiT, wT, k, vT, ki, topk=topk)
        conv = _conv(z, w_dw[l], b_dw[l], conv_ln_g[l], conv_ln_b[l], tm=512)
        x = _post(x, attn, conv, sga, sgb, g_m, sh_f, sc_f, g_f, norm_ffn_g[l], norm_final_g,
                  w_attn_proj[l], w_conv_proj[l], w_out[l], w_ffn_in[l], w_ffn_out[l], tm=512,
                  final_norm=(l == depth - 1))
    return x
```

```python
import functools
import math

import jax
import jax.numpy as jnp
import numpy as np
from jax import lax
from jax.experimental import pallas as pl
from jax.experimental.pallas import tpu as pltpu

N_HEADS = 8
HEAD_DIM = 64
ATTN_WIDTH = N_HEADS * HEAD_DIM
IDX_HEADS = 8
IDX_DIM = 64
TOPK_MAX = 256
CONV_CH = 512
CONV_WIDTH = 31
N_MOD = 6
EPS = 1e-6

LANES = 128
SUBLANES = 8
PACKED_ROWS = 16
CONV_HALO = 32
NEG_BIG = -1e30
VMEM_LIMIT = 56 * 1024 * 1024

ATT_TQ = 256
ATT_CK = 512
ATT_CC = 1024
SUM_ROWS = 16
FINE_CHECKS = (9, 11, 13, 15)
LOG2E = math.log2(math.e)

f32 = jnp.float32
bf16 = jnp.bfloat16
coarse_t = jnp.bfloat16


def _resident(block_shape, index_map):
    return pl.BlockSpec(block_shape, index_map, pipeline_mode=pl.Buffered(1))


REDUCE_CHAINS = 2


def _reduce_rows(op, x, rows):
    groups = [x[r * rows:(r + 1) * rows] for r in range(x.shape[0] // rows)]
    chains = groups[:REDUCE_CHAINS]
    for r, g in enumerate(groups[REDUCE_CHAINS:]):
        chains[r % REDUCE_CHAINS] = op(chains[r % REDUCE_CHAINS], g)
    while len(chains) > 1:
        chains = [op(chains[a], chains[a + 1]) for a in range(0, len(chains), 2)]
    return chains[0]


def _ada_kernel(c_ref, w_ref, b_ref, o_ref):
    c = c_ref[...]
    ca = c * jax.nn.sigmoid(c)
    o_ref[...] = jnp.dot(ca, w_ref[...], preferred_element_type=f32,
                         precision=lax.Precision.HIGHEST) + b_ref[...]


def _ada(c, w_ada, b_ada):
    B, D = c.shape
    N = w_ada.shape[1]
    rows = SUBLANES
    c_pad = jnp.zeros((rows, D), f32).at[:B].set(c)
    tn = 1024
    out = pl.pallas_call(
        _ada_kernel,
        out_shape=jax.ShapeDtypeStruct((rows, N), f32),
        grid=(N // tn,),
        in_specs=[pl.BlockSpec((rows, D), lambda j: (0, 0)),
                  pl.BlockSpec((D, tn), lambda j: (0, j)),
                  pl.BlockSpec((1, tn), lambda j: (0, j))],
        out_specs=pl.BlockSpec((rows, tn), lambda j: (0, j)),
        compiler_params=pltpu.CompilerParams(dimension_semantics=("arbitrary",)),
        name="ada",
    )(c_pad, w_ada, b_ada.reshape(1, N))
    return out[:B]


def _inproj_kernel(x_ref, sh_ref, sc_ref, g_ref, wt_ref, wk_ref, wki_ref, wwi_ref,
                   wua_ref, wug_ref, wga_ref, wgb_ref,
                   qT_ref, qiT_ref, vT_ref, k_ref, ki_ref, wT_ref, z_ref, sga_ref, sgb_ref,
                   *, ck):
    x = x_ref[0]
    ms = jnp.mean(x * x, axis=-1, keepdims=True)
    h = x * lax.rsqrt(ms + EPS) * g_ref[...]
    h = h * (1.0 + sc_ref[0]) + sh_ref[0]
    hb = h.astype(bf16)

    def proj(w_ref):
        return jnp.dot(hb, w_ref[...], preferred_element_type=f32)

    ptT = proj(wt_ref).T
    qT_ref[0] = ptT[0:ATTN_WIDTH].astype(bf16)
    qiT_ref[0] = ptT[ATTN_WIDTH:2 * ATTN_WIDTH].astype(bf16)
    tm = x.shape[0]
    for t in range(tm // ck):
        vT_ref[0, t] = ptT[2 * ATTN_WIDTH:3 * ATTN_WIDTH, t * ck:(t + 1) * ck].astype(bf16)
    k_ref[0] = proj(wk_ref).astype(bf16)
    ki_ref[0] = proj(wki_ref).astype(bf16)
    wiT = proj(wwi_ref).T
    wT_ref[0] = wiT[0:IDX_HEADS] * (IDX_HEADS ** -0.5)
    z_ref[0] = proj(wua_ref) * jax.nn.sigmoid(proj(wug_ref))
    sga_ref[0] = jax.nn.sigmoid(proj(wga_ref)).astype(bf16)
    sgb_ref[0] = jax.nn.sigmoid(proj(wgb_ref)).astype(bf16)


def _inproj(x, shift, scale, g, w_in, *, tm, ck):
    B, S, D = x.shape
    W = ATTN_WIDTH
    o = 0
    wq = w_in[:, o:o + W]; o += W
    wk = w_in[:, o:o + W]; o += W
    wv = w_in[:, o:o + W]; o += W
    wqi = w_in[:, o:o + IDX_HEADS * IDX_DIM]; o += IDX_HEADS * IDX_DIM
    wki = w_in[:, o:o + IDX_DIM]; o += IDX_DIM
    wwi = w_in[:, o:o + IDX_HEADS]; o += IDX_HEADS
    wua = w_in[:, o:o + CONV_CH]; o += CONV_CH
    wug = w_in[:, o:o + CONV_CH]; o += CONV_CH
    wga = w_in[:, o:o + D]; o += D
    wgb = w_in[:, o:o + D]; o += D
    wt = jnp.concatenate([wq * (HEAD_DIM ** -0.5 * LOG2E), wqi * (IDX_DIM ** -0.5), wv], axis=1).astype(bf16)
    pad = lambda w: jnp.pad(w, ((0, 0), (0, LANES - w.shape[1]))).astype(bf16)
    weights = [wt, wk.astype(bf16), pad(wki), pad(wwi), wua.astype(bf16), wug.astype(bf16),
               wga.astype(bf16), wgb.astype(bf16)]

    tile = lambda n: pl.BlockSpec((1, tm, n), lambda b, i: (b, i, 0))
    tileT = lambda n: pl.BlockSpec((1, n, tm), lambda b, i: (b, 0, i))
    vec = pl.BlockSpec((1, 1, D), lambda b, i: (b, 0, 0))
    out_shape = [
        jax.ShapeDtypeStruct((B, W, S), bf16),
        jax.ShapeDtypeStruct((B, W, S), bf16),
        jax.ShapeDtypeStruct((B, S // ck, W, ck), bf16),
        jax.ShapeDtypeStruct((B, S, W), bf16),
        jax.ShapeDtypeStruct((B, S, LANES), bf16),
        jax.ShapeDtypeStruct((B, IDX_HEADS, S), f32),
        jax.ShapeDtypeStruct((B, S, CONV_CH), f32),
        jax.ShapeDtypeStruct((B, S, D), bf16),
        jax.ShapeDtypeStruct((B, S, D), bf16),
    ]
    out_specs = [tileT(W), tileT(W),
                 pl.BlockSpec((1, tm // ck, W, ck), lambda b, i: (b, i, 0, 0)),
                 tile(W), tile(LANES), tileT(IDX_HEADS), tile(CONV_CH), tile(D), tile(D)]
    in_specs = [tile(D), vec, vec, pl.BlockSpec((1, D), lambda b, i: (0, 0))]
    in_specs += [_resident(w.shape, lambda b, i: (0, 0)) for w in weights]
    return pl.pallas_call(
        functools.partial(_inproj_kernel, ck=ck),
        out_shape=out_shape,
        grid=(B, S // tm),
        in_specs=in_specs,
        out_specs=out_specs,
        compiler_params=pltpu.CompilerParams(dimension_semantics=("arbitrary", "arbitrary"),
                                             vmem_limit_bytes=VMEM_LIMIT),
        name="inproj",
    )(x, shift, scale, g.reshape(1, D), *weights)


def _ordered_bits_to_float(u):
    key = u ^ jnp.int32(-2 ** 31)
    bits = key ^ ((key >> 31) & jnp.int32(0x7FFFFFFF))
    return lax.bitcast_convert_type(bits, f32)


SLOPE_PIECES = 4


def _slope_pieces(h):
    rem = LOG2E * 2.0 ** (-8.0 * (h + 1) / N_HEADS)
    pieces = []
    for _ in range(SLOPE_PIECES):
        piece = float(np.asarray(rem, np.float32).astype(jnp.bfloat16).astype(np.float32))
        pieces.append(piece)
        rem -= piece
    return pieces


def _attn_kernel(qT_ref, qiT_ref, wT_ref, k_ref, vT_ref, ki_ref, o_ref,
                 sc_ref, sb_ref, qpad_ref, qipad_ref, m_ref, acc_ref, pos_ref, mb_ref, s_ref, stage_ref,
                 tri_ref, *, topk):
    tq = qT_ref.shape[2]
    ck = vT_ref.shape[3]
    cc = ATT_CC
    i = pl.program_id(1)
    q0 = i * tq
    n_keys = q0 + tq
    n_ck = (n_keys + ck - 1) // ck
    n_cc = (n_keys + cc - 1) // cc

    lane1 = lax.broadcasted_iota(jnp.int32, (1, tq), 1)

    @pl.when(i == 0)
    def _():
        for c in range(pos_ref.shape[0] // ck):
            kpos = c * ck + lax.broadcasted_iota(jnp.int32, (ck, LANES), 0)
            lane = lax.broadcasted_iota(jnp.int32, (ck, LANES), 1)
            feat = jnp.where(lane < SLOPE_PIECES, kpos >> 7,
                             jnp.where(lane < 2 * SLOPE_PIECES, kpos & (LANES - 1), 0))
            pos_ref[c * ck:(c + 1) * ck, :] = feat.astype(f32).astype(bf16)
        rowi = lax.broadcasted_iota(jnp.int32, (LANES, tq), 0)
        for h in range(N_HEADS):
            coef = jnp.zeros((LANES, tq), f32)
            for p, piece in enumerate(_slope_pieces(h)):
                coef = jnp.where(rowi == p, LANES * piece, coef)
                coef = jnp.where(rowi == SLOPE_PIECES + p, piece, coef)
            qpad_ref[h, 2 * HEAD_DIM:2 * HEAD_DIM + LANES, :] = coef.astype(bf16)
        tri_ref[...] = jnp.where(lax.broadcasted_iota(jnp.int32, (tq, tq), 1)
                                 < lax.broadcasted_iota(jnp.int32, (tq, tq), 0), 1.0, 0.0).astype(bf16)

    qT = qT_ref[0]
    qiT = qiT_ref[0]
    first_half = lax.broadcasted_iota(jnp.int32, (2 * HEAD_DIM, tq), 0) < HEAD_DIM
    zeros_half = jnp.zeros((IDX_DIM, tq), bf16)
    for h in range(N_HEADS):
        pair = qT[(h // 2) * 2 * HEAD_DIM:(h // 2 + 1) * 2 * HEAD_DIM]
        keep = first_half if h % 2 == 0 else jnp.logical_not(first_half)
        qpad_ref[h, 0:2 * HEAD_DIM, :] = jnp.where(keep, pair, jnp.zeros_like(pair))
    for h in range(IDX_HEADS):
        qipad_ref[h] = jnp.concatenate([qiT[h * IDX_DIM:(h + 1) * IDX_DIM], zeros_half], axis=0)

    w_rows = wT_ref[0]
    row_t = lax.broadcasted_iota(jnp.int32, (tq, tq), 0)
    lane_t = lax.broadcasted_iota(jnp.int32, (tq, tq), 1)

    def score_chunk(j, carry):
        for t in range(ck // tq):
            k0 = pl.multiple_of(j * ck + t * tq, tq)
            kit = ki_ref[0, pl.ds(k0, tq), :]
            acc = jnp.zeros((tq, tq), f32)
            for h in range(IDX_HEADS):
                lg = jnp.dot(kit, qipad_ref[h], preferred_element_type=f32)
                acc = acc + w_rows[h:h + 1, :] * jnp.maximum(lg, 0.0)
            causal = (k0 + row_t) <= (q0 + lane_t)
            acc = jnp.where(causal, acc, -jnp.inf)
            sc_ref[pl.ds(k0, tq), :] = acc
            sb_ref[pl.ds(k0, tq), :] = acc.astype(coarse_t)
            n_pos, n_nonneg = carry
            carry = (n_pos + _reduce_rows(jnp.add, jnp.where(acc > 0.0, 1.0, 0.0), SUBLANES),
                     n_nonneg + _reduce_rows(jnp.add, jnp.where(acc >= 0.0, 1.0, 0.0), SUBLANES))
        return carry

    n_pos, n_nonneg = lax.fori_loop(0, n_ck, score_chunk,
                                    (jnp.zeros((SUBLANES, tq), f32), jnp.zeros((SUBLANES, tq), f32)))
    n_pos = n_pos.sum(axis=0, keepdims=True)
    n_nonneg = n_nonneg.sum(axis=0, keepdims=True)

    @pl.when(n_ck * ck < n_cc * cc)
    def _():
        for r0 in range(ck, cc, ck):
            start = pl.multiple_of(n_ck * ck + (r0 - ck), ck)
            sc_ref[pl.ds(start, ck), :] = jnp.full((ck, tq), -jnp.inf, f32)
            sb_ref[pl.ds(start, ck), :] = jnp.full((ck, tq), -jnp.inf, coarse_t)

    kf = jnp.minimum(topk, q0 + lane1 + 1).astype(f32)

    def count(pred):
        def body(j, acc):
            k0 = pl.multiple_of(j * cc, cc)
            hit = jnp.where(pred(sc_ref[pl.ds(k0, cc), :]), 1.0, 0.0)
            return acc + _reduce_rows(jnp.add, hit, SUBLANES)
        acc = lax.fori_loop(0, n_cc, body, jnp.zeros((SUBLANES, tq), f32))
        return acc.sum(axis=0, keepdims=True)

    def count_coarse(thr16):
        one = jnp.ones((), coarse_t)
        zero = jnp.zeros((), coarse_t)

        def body(j, acc):
            k0 = pl.multiple_of(j * cc, cc)
            hit = jnp.where(sb_ref[pl.ds(k0, cc), :] >= thr16, one, zero)
            return acc + _reduce_rows(jnp.add, hit, PACKED_ROWS).astype(f32)
        acc = lax.fori_loop(0, n_cc, body, jnp.zeros((PACKED_ROWS, tq), f32))
        return acc.sum(axis=0, keepdims=True)

    def coarse_value_bits(v):
        return (v ^ jnp.where((v & 0x8000) != 0, 0x8000, 0xFFFF)) << 16

    def bisect_coarse(it, v):
        cand = v | lax.shift_left(jnp.int32(1), 15 - it)
        thr16 = lax.bitcast_convert_type(coarse_value_bits(cand), f32).astype(coarse_t)
        return jnp.where(count_coarse(thr16) >= kf, cand, v)

    v = lax.fori_loop(0, 16, bisect_coarse, jnp.zeros((1, tq), jnp.int32))
    vbits = coarse_value_bits(v)
    u_mid = vbits ^ jnp.where(vbits < 0, jnp.int32(-1), jnp.int32(-2 ** 31))
    u_base = u_mid - 0x8000
    fine_bits = 17

    def fine_body(it, c):
        d, done = c
        cand = d | lax.shift_left(jnp.int32(1), fine_bits - 1 - it)
        thr_c = _ordered_bits_to_float(u_base + cand)
        n = count(lambda s: s >= thr_c)
        d = jnp.where(jnp.logical_and(done < 0.5, n >= kf), cand, d)
        done = jnp.maximum(done, jnp.where(n == kf, 1.0, 0.0))
        return d, done

    def all_done(c):
        return jnp.min(c[1]) > 0.5

    zero_tie = jnp.logical_and(n_pos < kf, kf <= n_nonneg)
    tied0 = jnp.where(zero_tie, 1.0, 0.0)

    c = lax.fori_loop(0, FINE_CHECKS[0], fine_body, (jnp.zeros((1, tq), jnp.int32), tied0))
    for lo, hi in zip(FINE_CHECKS, FINE_CHECKS[1:] + (fine_bits,)):
        c = lax.cond(all_done(c), lambda c: c,
                     functools.partial(lax.fori_loop, lo, hi, fine_body), c)
    d = c[0]
    u = jnp.where(zero_tie, jnp.int32(-2 ** 31), u_base + d)
    thr = _ordered_bits_to_float(u)
    thr_next = _ordered_bits_to_float(u + 1)
    next_bits = lax.bitcast_convert_type(thr_next, jnp.int32)
    thr_next = jnp.where(jnp.logical_and(next_bits > 0, next_bits < 0x00800000),
                         float(np.finfo(np.float32).tiny), thr_next)

    @pl.when(jnp.logical_or(jnp.max(tied0) > 0.5, jnp.logical_not(all_done(c))))
    def _():
        need = kf - count(lambda s: s >= thr_next)

        def drop_chunk(j, before):
            k0 = pl.multiple_of(j * cc, cc)
            s_all = sc_ref[pl.ds(k0, cc), :]
            kept = []
            for t in range(cc // tq):
                s = s_all[t * tq:(t + 1) * tq]
                tie = (s >= thr) & (s < thr_next)
                tie01 = jnp.where(tie, 1.0, 0.0)
                within = jnp.dot(tri_ref[...], tie01.astype(bf16), preferred_element_type=f32)
                kept.append(jnp.where(tie & (within >= need - before), -jnp.inf, s))
                before = before + _reduce_rows(jnp.add, tie01, SUBLANES).sum(axis=0, keepdims=True)
            sc_ref[pl.ds(k0, cc), :] = jnp.concatenate(kept, axis=0)
            return before

        lax.fori_loop(0, n_cc, drop_chunk, jnp.zeros((1, tq), f32))

    m_ref[...] = jnp.full(m_ref.shape, NEG_BIG, f32)
    acc_ref[...] = jnp.zeros(acc_ref.shape, f32)

    ones_rows = jnp.ones((SUM_ROWS, ck), bf16)

    def logits_head(j, buf, h):
        k0 = pl.multiple_of(j * ck, ck)
        kp = k_ref[0, pl.ds(k0, ck), (h // 2) * LANES:(h // 2 + 1) * LANES]
        s = jnp.dot(jnp.concatenate([kp, pos_ref[pl.ds(k0, ck), :]], axis=1), qpad_ref[h],
                    preferred_element_type=f32)
        s = s + mb_ref[buf]
        s_ref[buf, h] = s
        mt = _reduce_rows(jnp.maximum, s, SUBLANES).max(axis=0, keepdims=True)
        m_old = m_ref[h:h + 1, :]
        m_new = jnp.maximum(m_old, mt)
        m_ref[h:h + 1, :] = m_new
        stage_ref[buf, 0, h:h + 1, :] = m_new
        stage_ref[buf, 1, h:h + 1, :] = jnp.exp2(m_old - m_new)

    def probs_head(j, buf, h):
        p = jnp.exp2(s_ref[buf, h] - stage_ref[buf, 0, h:h + 1, :]).astype(bf16)
        vt = jnp.concatenate([vT_ref[0, j, h * HEAD_DIM:(h + 1) * HEAD_DIM, :], ones_rows], axis=0)
        pv = jnp.dot(vt, p, preferred_element_type=f32)
        acc_ref[h] = stage_ref[buf, 1, h:h + 1, :] * acc_ref[h] + pv

    def mask_step(j, buf):
        k0 = pl.multiple_of(j * ck, ck)
        mb_ref[buf] = jnp.where(sc_ref[pl.ds(k0, ck), :] >= thr, 0.0, NEG_BIG)

    def logits_step(j, buf):
        mask_step(j, buf)
        for h in range(N_HEADS):
            logits_head(j, buf, h)

    def probs_step(j, buf):
        for h in range(N_HEADS):
            probs_head(j, buf, h)

    def overlapped(jp, bufp, jl, bufl):
        mask_step(jl, bufl)
        for h in range(N_HEADS):
            logits_head(jl, bufl, h)
            probs_head(jp, bufp, h)

    logits_step(0, 0)

    def chunk_pair(t, carry):
        overlapped(2 * t, 0, 2 * t + 1, 1)
        overlapped(2 * t + 1, 1, 2 * t + 2, 0)
        return carry

    n_pairs = (n_ck - 1) // 2
    lax.fori_loop(0, n_pairs, chunk_pair, 0)
    probs_step(2 * n_pairs, 0)

    @pl.when(2 * n_pairs + 1 < n_ck)
    def _():
        logits_step(n_ck - 1, 1)
        probs_step(n_ck - 1, 1)

    outs = []
    for h in range(N_HEADS):
        outs.append(acc_ref[h, 0:HEAD_DIM, :] / acc_ref[h, HEAD_DIM:HEAD_DIM + 1, :])
    o_ref[0] = jnp.concatenate(outs, axis=0).T.astype(bf16)


def _attention(qT, qiT, wT, k, vT, ki, *, topk):
    B, W, S = qT.shape
    tq, ck, cc = ATT_TQ, ATT_CK, ATT_CC
    assert S % cc == 0 and cc == 2 * ck and ck % tq == 0 and vT.shape[3] == ck
    blkT = lambda n: pl.BlockSpec((1, n, tq), lambda b, i: (b, 0, i))
    return pl.pallas_call(
        functools.partial(_attn_kernel, topk=topk),
        out_shape=jax.ShapeDtypeStruct((B, S, W), bf16),
        grid=(B, S // tq),
        in_specs=[blkT(W), blkT(W), blkT(IDX_HEADS),
                  _resident((1, S, W), lambda b, i: (b, 0, 0)),
                  _resident((1, S // ck, W, ck), lambda b, i: (b, 0, 0, 0)),
                  _resident((1, S, LANES), lambda b, i: (b, 0, 0))],
        out_specs=pl.BlockSpec((1, tq, W), lambda b, i: (b, i, 0)),
        scratch_shapes=[pltpu.VMEM((S, tq), f32),
                        pltpu.VMEM((S, tq), coarse_t),
                        pltpu.VMEM((N_HEADS, 2 * HEAD_DIM + LANES, tq), bf16),
                        pltpu.VMEM((IDX_HEADS, 2 * IDX_DIM, tq), bf16),
                        pltpu.VMEM((N_HEADS, tq), f32),
                        pltpu.VMEM((N_HEADS, HEAD_DIM + SUM_ROWS, tq), f32),
                        pltpu.VMEM((S, LANES), bf16),
                        pltpu.VMEM((2, ck, tq), f32),
                        pltpu.VMEM((2, N_HEADS, ck, tq), f32),
                        pltpu.VMEM((2, 2, N_HEADS, tq), f32),
                        pltpu.VMEM((tq, tq), bf16)],
        compiler_params=pltpu.CompilerParams(dimension_semantics=("arbitrary", "arbitrary"),
                                             vmem_limit_bytes=VMEM_LIMIT),
        name="dsa_attn",
    )(qT, qiT, wT, k, vT, ki)


def _conv_kernel(zc_ref, zp_ref, w_ref, b_ref, g_ref, beta_ref, o_ref, ext_ref):
    tm = zc_ref.shape[1]
    i = pl.program_id(1)
    prev = jnp.where(i > 0, zp_ref[0], 0.0)
    ext_ref[0:CONV_HALO, :] = prev
    ext_ref[CONV_HALO:CONV_HALO + tm, :] = zc_ref[0]
    n_ext = CONV_HALO + tm
    ext = ext_ref[...]
    acc = jnp.broadcast_to(b_ref[...], (tm, CONV_CH))
    first = CONV_HALO - (CONV_WIDTH - 1)
    for b in range(SUBLANES):
        offs = [o for o in range(first, first + CONV_WIDTH) if o % SUBLANES == b]
        if not offs:
            continue
        shifted = ext if b == 0 else pltpu.roll(ext, n_ext - b, axis=0)
        for o in offs:
            a = o - b
            acc = acc + w_ref[o - first:o - first + 1, :] * shifted[a:a + tm]
    mu = jnp.mean(acc, axis=-1, keepdims=True)
    d = acc - mu
    var = jnp.mean(d * d, axis=-1, keepdims=True)
    y = d * lax.rsqrt(var + EPS) * g_ref[...] + beta_ref[...]
    o_ref[0] = (y * jax.nn.sigmoid(y)).astype(bf16)


def _conv(z, w_dw, b_dw, ln_g, ln_b, *, tm):
    B, S, C = z.shape
    r = tm // CONV_HALO
    vec = pl.BlockSpec((1, C), lambda b, i: (0, 0))
    return pl.pallas_call(
        _conv_kernel,
        out_shape=jax.ShapeDtypeStruct((B, S, C), bf16),
        grid=(B, S // tm),
        in_specs=[pl.BlockSpec((1, tm, C), lambda b, i: (b, i, 0)),
                  pl.BlockSpec((1, CONV_HALO, C), lambda b, i: (b, jnp.maximum(i * r - 1, 0), 0)),
                  pl.BlockSpec((CONV_WIDTH, C), lambda b, i: (0, 0)),
                  vec, vec, vec],
        out_specs=pl.BlockSpec((1, tm, C), lambda b, i: (b, i, 0)),
        scratch_shapes=[pltpu.VMEM((CONV_HALO + tm, C), f32)],
        compiler_params=pltpu.CompilerParams(dimension_semantics=("arbitrary", "arbitrary")),
        name="conv",
    )(z, z, w_dw.reshape(CONV_WIDTH, C), b_dw.reshape(1, C), ln_g.reshape(1, C), ln_b.reshape(1, C))


def _post_kernel(x_ref, attn_ref, conv_ref, sga_ref, sgb_ref, gm_ref, shf_ref, scf_ref, gf_ref,
                 gffn_ref, gfin_ref, wap_ref, wcp_ref, wout_ref, wfa_ref, wfb_ref, wfo_ref, o_ref,
                 *, final_norm):
    dot = lambda a, w_ref: jnp.dot(a, w_ref[...], preferred_element_type=f32)
    ya = dot(attn_ref[0], wap_ref)
    yc = dot(conv_ref[0], wcp_ref)
    merged = sga_ref[0].astype(f32) * ya + sgb_ref[0].astype(f32) * yc
    x1 = x_ref[0] + gm_ref[0] * dot(merged.astype(bf16), wout_ref)

    ms = jnp.mean(x1 * x1, axis=-1, keepdims=True)
    h = x1 * lax.rsqrt(ms + EPS) * gffn_ref[...]
    hb = (h * (1.0 + scf_ref[0]) + shf_ref[0]).astype(bf16)
    a = dot(hb, wfa_ref)
    b = dot(hb, wfb_ref)
    act = (a * jax.nn.sigmoid(a) * b).astype(bf16)
    x2 = x1 + gf_ref[0] * dot(act, wfo_ref)

    if final_norm:
        ms2 = jnp.mean(x2 * x2, axis=-1, keepdims=True)
        x2 = x2 * lax.rsqrt(ms2 + EPS) * gfin_ref[...]
    o_ref[0] = x2


def _post(x, attn, conv, sga, sgb, g_m, sh_f, sc_f, g_f, norm_ffn_g, norm_final_g,
          w_attn_proj, w_conv_proj, w_out, w_ffn_in, w_ffn_out, *, tm, final_norm):
    B, S, D = x.shape
    d_ff = w_ffn_out.shape[0]
    weights = [w_attn_proj.astype(bf16), w_conv_proj.astype(bf16), w_out.astype(bf16),
               w_ffn_in[:, :d_ff].astype(bf16), w_ffn_in[:, d_ff:].astype(bf16), w_ffn_out.astype(bf16)]
    tile = lambda n: pl.BlockSpec((1, tm, n), lambda b, i: (b, i, 0))
    vec = pl.BlockSpec((1, 1, D), lambda b, i: (b, 0, 0))
    gain = pl.BlockSpec((1, D), lambda b, i: (0, 0))
    in_specs = [tile(D), tile(ATTN_WIDTH), tile(CONV_CH), tile(D), tile(D), vec, vec, vec, vec, gain, gain]
    in_specs += [_resident(w.shape, lambda b, i: (0, 0)) for w in weights]
    return pl.pallas_call(
        functools.partial(_post_kernel, final_norm=final_norm),
        out_shape=jax.ShapeDtypeStruct((B, S, D), f32),
        grid=(B, S // tm),
        in_specs=in_specs,
        out_specs=tile(D),
        compiler_params=pltpu.CompilerParams(dimension_semantics=("arbitrary", "arbitrary"),
                                             vmem_limit_bytes=VMEM_LIMIT),
        name="post",
    )(x, attn, conv, sga, sgb, g_m, sh_f, sc_f, g_f, norm_ffn_g.reshape(1, D),
      norm_final_g.reshape(1, D), *weights)


def kernel(x, c, norm_mix_g, w_in, w_dw, b_dw, conv_ln_g, conv_ln_b, w_attn_proj, w_conv_proj,
           w_out, norm_ffn_g, w_ffn_in, w_ffn_out, w_ada, b_ada, norm_final_g):
    B, S, D = x.shape
    depth = w_in.shape[0]
    topk = min(TOPK_MAX, S // 4)
    c = c.astype(f32)
    for l in range(depth):
        mod = _ada(c, w_ada[l], b_ada[l])
        sh_m, sc_m, g_m, sh_f, sc_f, g_f = [m.reshape(B, 1, D) for m in jnp.split(mod, N_MOD, axis=-1)]
        qT, qiT, vT, k, ki, wT, z, sga, sgb = _inproj(x, sh_m, sc_m, norm_mix_g[l], w_in[l],
                                                      tm=512, ck=ATT_CK)
        attn = _attention(qT, qiT, wT, k, vT, ki, topk=topk)
        conv = _conv(z, w_dw[l], b_dw[l], conv_ln_g[l], conv_ln_b[l], tm=512)
        x = _post(x, attn, conv, sga, sgb, g_m, sh_f, sc_f, g_f, norm_ffn_g[l], norm_final_g,
                  w_attn_proj[l], w_conv_proj[l], w_out[l], w_ffn_in[l], w_ffn_out[l], tm=512,
                  final_norm=(l == depth - 1))
    return x
```

```python
import functools
import math

import jax
import jax.numpy as jnp
import numpy as np
from jax import lax
from jax.experimental import pallas as pl
from jax.experimental.pallas import tpu as pltpu

N_HEADS = 8
HEAD_DIM = 64
ATTN_WIDTH = N_HEADS * HEAD_DIM
IDX_HEADS = 8
IDX_DIM = 64
TOPK_MAX = 256
CONV_CH = 512
CONV_WIDTH = 31
N_MOD = 6
EPS = 1e-6

LANES = 128
SUBLANES = 8
PACKED_ROWS = 16
CONV_HALO = 32
NEG_BIG = -1e30
VMEM_LIMIT = 56 * 1024 * 1024

ATT_TQ = 256
ATT_CK = 512
ATT_CC = 1024
SUM_ROWS = 16
FINE_CHECKS = (9, 11, 13, 15)
LOG2E = math.log2(math.e)

f32 = jnp.float32
bf16 = jnp.bfloat16
coarse_t = jnp.bfloat16


def _resident(block_shape, index_map):
    return pl.BlockSpec(block_shape, index_map, pipeline_mode=pl.Buffered(1))


REDUCE_CHAINS = 2


def _reduce_rows(op, x, rows):
    groups = [x[r * rows:(r + 1) * rows] for r in range(x.shape[0] // rows)]
    chains = groups[:REDUCE_CHAINS]
    for r, g in enumerate(groups[REDUCE_CHAINS:]):
        chains[r % REDUCE_CHAINS] = op(chains[r % REDUCE_CHAINS], g)
    while len(chains) > 1:
        chains = [op(chains[a], chains[a + 1]) for a in range(0, len(chains), 2)]
    return chains[0]


def _ada_kernel(c_ref, w_ref, b_ref, o_ref):
    c = c_ref[...]
    ca = c * jax.nn.sigmoid(c)
    o_ref[...] = jnp.dot(ca, w_ref[...], preferred_element_type=f32,
                         precision=lax.Precision.HIGHEST) + b_ref[...]


def _ada(c, w_ada, b_ada):
    B, D = c.shape
    N = w_ada.shape[1]
    rows = SUBLANES
    c_pad = jnp.zeros((rows, D), f32).at[:B].set(c)
    tn = 1024
    out = pl.pallas_call(
        _ada_kernel,
        out_shape=jax.ShapeDtypeStruct((rows, N), f32),
        grid=(N // tn,),
        in_specs=[pl.BlockSpec((rows, D), lambda j: (0, 0)),
                  pl.BlockSpec((D, tn), lambda j: (0, j)),
                  pl.BlockSpec((1, tn), lambda j: (0, j))],
        out_specs=pl.BlockSpec((rows, tn), lambda j: (0, j)),
        compiler_params=pltpu.CompilerParams(dimension_semantics=("arbitrary",)),
        name="ada",
    )(c_pad, w_ada, b_ada.reshape(1, N))
    return out[:B]


def _conv_ln_swish(ext, tm, w_ref, b_ref, g_ref, beta_ref):
    n_ext = ext.shape[0]
    acc = jnp.broadcast_to(b_ref[...], (tm, CONV_CH))
    first = CONV_HALO - (CONV_WIDTH - 1)
    for b in range(SUBLANES):
        offs = [o for o in range(first, first + CONV_WIDTH) if o % SUBLANES == b]
        if not offs:
            continue
        shifted = ext if b == 0 else pltpu.roll(ext, n_ext - b, axis=0)
        for o in offs:
            a = o - b
            acc = acc + w_ref[o - first:o - first + 1, :] * shifted[a:a + tm]
    mu = jnp.mean(acc, axis=-1, keepdims=True)
    d = acc - mu
    var = jnp.mean(d * d, axis=-1, keepdims=True)
    y = d * lax.rsqrt(var + EPS) * g_ref[...] + beta_ref[...]
    return (y * jax.nn.sigmoid(y)).astype(bf16)


def _inproj_kernel(x_ref, sh_ref, sc_ref, g_ref, wt_ref, wk_ref, wki_ref, wwi_ref,
                   wua_ref, wug_ref, wga_ref, wgb_ref, wdw_ref, bdw_ref, lng_ref, lnb_ref,
                   qT_ref, qiT_ref, vT_ref, k_ref, ki_ref, wT_ref, conv_ref, sga_ref, sgb_ref,
                   ext_ref, *, ck):
    @pl.when(pl.program_id(1) == 0)
    def _():
        ext_ref[0:CONV_HALO, :] = jnp.zeros((CONV_HALO, CONV_CH), f32)

    x = x_ref[0]
    tm = x.shape[0]
    ms = jnp.mean(x * x, axis=-1, keepdims=True)
    h = x * lax.rsqrt(ms + EPS) * g_ref[...]
    h = h * (1.0 + sc_ref[0]) + sh_ref[0]
    hb = h.astype(bf16)

    def proj(w_ref):
        return jnp.dot(hb, w_ref[...], preferred_element_type=f32)

    z = proj(wua_ref) * jax.nn.sigmoid(proj(wug_ref))
    ext_ref[CONV_HALO:CONV_HALO + tm, :] = z
    conv_ref[0] = _conv_ln_swish(ext_ref[...], tm, wdw_ref, bdw_ref, lng_ref, lnb_ref)
    ext_ref[0:CONV_HALO, :] = z[tm - CONV_HALO:tm]

    ptT = proj(wt_ref).T
    qT_ref[0] = ptT[0:ATTN_WIDTH].astype(bf16)
    qiT_ref[0] = ptT[ATTN_WIDTH:2 * ATTN_WIDTH].astype(bf16)
    for t in range(tm // ck):
        vT_ref[0, t] = ptT[2 * ATTN_WIDTH:3 * ATTN_WIDTH, t * ck:(t + 1) * ck].astype(bf16)
    k_ref[0] = proj(wk_ref).astype(bf16)
    ki_ref[0] = proj(wki_ref).astype(bf16)
    wiT = proj(wwi_ref).T
    wT_ref[0] = wiT[0:IDX_HEADS] * (IDX_HEADS ** -0.5)
    sga_ref[0] = jax.nn.sigmoid(proj(wga_ref)).astype(bf16)
    sgb_ref[0] = jax.nn.sigmoid(proj(wgb_ref)).astype(bf16)


def _inproj(x, shift, scale, g, w_in, w_dw, b_dw, ln_g, ln_b, *, tm, ck):
    B, S, D = x.shape
    C = CONV_CH
    conv_params = [w_dw.reshape(CONV_WIDTH, C), b_dw.reshape(1, C), ln_g.reshape(1, C), ln_b.reshape(1, C)]
    W = ATTN_WIDTH
    o = 0
    wq = w_in[:, o:o + W]; o += W
    wk = w_in[:, o:o + W]; o += W
    wv = w_in[:, o:o + W]; o += W
    wqi = w_in[:, o:o + IDX_HEADS * IDX_DIM]; o += IDX_HEADS * IDX_DIM
    wki = w_in[:, o:o + IDX_DIM]; o += IDX_DIM
    wwi = w_in[:, o:o + IDX_HEADS]; o += IDX_HEADS
    wua = w_in[:, o:o + CONV_CH]; o += CONV_CH
    wug = w_in[:, o:o + CONV_CH]; o += CONV_CH
    wga = w_in[:, o:o + D]; o += D
    wgb = w_in[:, o:o + D]; o += D
    wt = jnp.concatenate([wq * (HEAD_DIM ** -0.5 * LOG2E), wqi * (IDX_DIM ** -0.5), wv], axis=1).astype(bf16)
    pad = lambda w: jnp.pad(w, ((0, 0), (0, LANES - w.shape[1]))).astype(bf16)
    weights = [wt, wk.astype(bf16), pad(wki), pad(wwi), wua.astype(bf16), wug.astype(bf16),
               wga.astype(bf16), wgb.astype(bf16)]

    tile = lambda n: pl.BlockSpec((1, tm, n), lambda b, i: (b, i, 0))
    tileT = lambda n: pl.BlockSpec((1, n, tm), lambda b, i: (b, 0, i))
    vec = pl.BlockSpec((1, 1, D), lambda b, i: (b, 0, 0))
    out_shape = [
        jax.ShapeDtypeStruct((B, W, S), bf16),
        jax.ShapeDtypeStruct((B, W, S), bf16),
        jax.ShapeDtypeStruct((B, S // ck, W, ck), bf16),
        jax.ShapeDtypeStruct((B, S, W), bf16),
        jax.ShapeDtypeStruct((B, S, LANES), bf16),
        jax.ShapeDtypeStruct((B, IDX_HEADS, S), f32),
        jax.ShapeDtypeStruct((B, S, CONV_CH), bf16),
        jax.ShapeDtypeStruct((B, S, D), bf16),
        jax.ShapeDtypeStruct((B, S, D), bf16),
    ]
    out_specs = [tileT(W), tileT(W),
                 pl.BlockSpec((1, tm // ck, W, ck), lambda b, i: (b, i, 0, 0)),
                 tile(W), tile(LANES), tileT(IDX_HEADS), tile(CONV_CH), tile(D), tile(D)]
    in_specs = [tile(D), vec, vec, pl.BlockSpec((1, D), lambda b, i: (0, 0))]
    in_specs += [_resident(w.shape, lambda b, i: (0, 0)) for w in weights]
    in_specs += [pl.BlockSpec(p.shape, lambda b, i: (0, 0)) for p in conv_params]
    return pl.pallas_call(
        functools.partial(_inproj_kernel, ck=ck),
        out_shape=out_shape,
        grid=(B, S // tm),
        in_specs=in_specs,
        out_specs=out_specs,
        scratch_shapes=[pltpu.VMEM((CONV_HALO + tm, C), f32)],
        compiler_params=pltpu.CompilerParams(dimension_semantics=("arbitrary", "arbitrary"),
                                             vmem_limit_bytes=VMEM_LIMIT),
        name="inproj",
    )(x, shift, scale, g.reshape(1, D), *weights, *conv_params)


def _ordered_bits_to_float(u):
    key = u ^ jnp.int32(-2 ** 31)
    bits = key ^ ((key >> 31) & jnp.int32(0x7FFFFFFF))
    return lax.bitcast_convert_type(bits, f32)


SLOPE_PIECES = 4


def _slope_pieces(h):
    rem = LOG2E * 2.0 ** (-8.0 * (h + 1) / N_HEADS)
    pieces = []
    for _ in range(SLOPE_PIECES):
        piece = float(np.asarray(rem, np.float32).astype(jnp.bfloat16).astype(np.float32))
        pieces.append(piece)
        rem -= piece
    return pieces


def _attn_kernel(qT_ref, qiT_ref, wT_ref, k_ref, vT_ref, ki_ref, o_ref,
                 sc_ref, sb_ref, qpad_ref, qipad_ref, m_ref, acc_ref, pos_ref, mb_ref, s_ref, stage_ref,
                 tri_ref, *, topk):
    tq = qT_ref.shape[2]
    ck = vT_ref.shape[3]
    cc = ATT_CC
    i = pl.program_id(1)
    q0 = i * tq
    n_keys = q0 + tq
    n_ck = (n_keys + ck - 1) // ck
    n_cc = (n_keys + cc - 1) // cc

    lane1 = lax.broadcasted_iota(jnp.int32, (1, tq), 1)

    @pl.when(i == 0)
    def _():
        for c in range(pos_ref.shape[0] // ck):
            kpos = c * ck + lax.broadcasted_iota(jnp.int32, (ck, LANES), 0)
            lane = lax.broadcasted_iota(jnp.int32, (ck, LANES), 1)
            feat = jnp.where(lane < SLOPE_PIECES, kpos >> 7,
                             jnp.where(lane < 2 * SLOPE_PIECES, kpos & (LANES - 1), 0))
            pos_ref[c * ck:(c + 1) * ck, :] = feat.astype(f32).astype(bf16)
        rowi = lax.broadcasted_iota(jnp.int32, (LANES, tq), 0)
        for h in range(N_HEADS):
            coef = jnp.zeros((LANES, tq), f32)
            for p, piece in enumerate(_slope_pieces(h)):
                coef = jnp.where(rowi == p, LANES * piece, coef)
                coef = jnp.where(rowi == SLOPE_PIECES + p, piece, coef)
            qpad_ref[h, 2 * HEAD_DIM:2 * HEAD_DIM + LANES, :] = coef.astype(bf16)
        tri_ref[...] = jnp.where(lax.broadcasted_iota(jnp.int32, (tq, tq), 1)
                                 < lax.broadcasted_iota(jnp.int32, (tq, tq), 0), 1.0, 0.0).astype(bf16)

    qT = qT_ref[0]
    qiT = qiT_ref[0]
    first_half = lax.broadcasted_iota(jnp.int32, (2 * HEAD_DIM, tq), 0) < HEAD_DIM
    zeros_half = jnp.zeros((IDX_DIM, tq), bf16)
    for h in range(N_HEADS):
        pair = qT[(h // 2) * 2 * HEAD_DIM:(h // 2 + 1) * 2 * HEAD_DIM]
        keep = first_half if h % 2 == 0 else jnp.logical_not(first_half)
        qpad_ref[h, 0:2 * HEAD_DIM, :] = jnp.where(keep, pair, jnp.zeros_like(pair))
    for h in range(IDX_HEADS):
        qipad_ref[h] = jnp.concatenate([qiT[h * IDX_DIM:(h + 1) * IDX_DIM], zeros_half], axis=0)

    w_rows = wT_ref[0]
    row_t = lax.broadcasted_iota(jnp.int32, (tq, tq), 0)
    lane_t = lax.broadcasted_iota(jnp.int32, (tq, tq), 1)

    def score_chunk(j, carry):
        for t in range(ck // tq):
            k0 = pl.multiple_of(j * ck + t * tq, tq)
            kit = ki_ref[0, pl.ds(k0, tq), :]
            acc = jnp.zeros((tq, tq), f32)
            for h in range(IDX_HEADS):
                lg = jnp.dot(kit, qipad_ref[h], preferred_element_type=f32)
                acc = acc + w_rows[h:h + 1, :] * jnp.maximum(lg, 0.0)
            causal = (k0 + row_t) <= (q0 + lane_t)
            acc = jnp.where(causal, acc, -jnp.inf)
            sc_ref[pl.ds(k0, tq), :] = acc
            sb_ref[pl.ds(k0, tq), :] = acc.astype(coarse_t)
            n_pos, n_nonneg = carry
            carry = (n_pos + _reduce_rows(jnp.add, jnp.where(acc > 0.0, 1.0, 0.0), SUBLANES),
                     n_nonneg + _reduce_rows(jnp.add, jnp.where(acc >= 0.0, 1.0, 0.0), SUBLANES))
        return carry

    n_pos, n_nonneg = lax.fori_loop(0, n_ck, score_chunk,
                                    (jnp.zeros((SUBLANES, tq), f32), jnp.zeros((SUBLANES, tq), f32)))
    n_pos = n_pos.sum(axis=0, keepdims=True)
    n_nonneg = n_nonneg.sum(axis=0, keepdims=True)

    @pl.when(n_ck * ck < n_cc * cc)
    def _():
        for r0 in range(ck, cc, ck):
            start = pl.multiple_of(n_ck * ck + (r0 - ck), ck)
            sc_ref[pl.ds(start, ck), :] = jnp.full((ck, tq), -jnp.inf, f32)
            sb_ref[pl.ds(start, ck), :] = jnp.full((ck, tq), -jnp.inf, coarse_t)

    kf = jnp.minimum(topk, q0 + lane1 + 1).astype(f32)

    def count(pred):
        def body(j, acc):
            k0 = pl.multiple_of(j * cc, cc)
            hit = jnp.where(pred(sc_ref[pl.ds(k0, cc), :]), 1.0, 0.0)
            return acc + _reduce_rows(jnp.add, hit, SUBLANES)
        acc = lax.fori_loop(0, n_cc, body, jnp.zeros((SUBLANES, tq), f32))
        return acc.sum(axis=0, keepdims=True)

    def count_coarse(thr16):
        one = jnp.ones((), coarse_t)
        zero = jnp.zeros((), coarse_t)

        def body(j, acc):
            k0 = pl.multiple_of(j * cc, cc)
            hit = jnp.where(sb_ref[pl.ds(k0, cc), :] >= thr16, one, zero)
            return acc + _reduce_rows(jnp.add, hit, PACKED_ROWS).astype(f32)
        acc = lax.fori_loop(0, n_cc, body, jnp.zeros((PACKED_ROWS, tq), f32))
        return acc.sum(axis=0, keepdims=True)

    def coarse_value_bits(v):
        return (v ^ jnp.where((v & 0x8000) != 0, 0x8000, 0xFFFF)) << 16

    def bisect_coarse(it, v):
        cand = v | lax.shift_left(jnp.int32(1), 15 - it)
        thr16 = lax.bitcast_convert_type(coarse_value_bits(cand), f32).astype(coarse_t)
        return jnp.where(count_coarse(thr16) >= kf, cand, v)

    v = lax.fori_loop(0, 16, bisect_coarse, jnp.zeros((1, tq), jnp.int32))
    vbits = coarse_value_bits(v)
    u_mid = vbits ^ jnp.where(vbits < 0, jnp.int32(-1), jnp.int32(-2 ** 31))
    u_base = u_mid - 0x8000
    fine_bits = 17

    def fine_body(it, c):
        d, done = c
        cand = d | lax.shift_left(jnp.int32(1), fine_bits - 1 - it)
        thr_c = _ordered_bits_to_float(u_base + cand)
        n = count(lambda s: s >= thr_c)
        d = jnp.where(jnp.logical_and(done < 0.5, n >= kf), cand, d)
        done = jnp.maximum(done, jnp.where(n == kf, 1.0, 0.0))
        return d, done

    def all_done(c):
        return jnp.min(c[1]) > 0.5

    zero_tie = jnp.logical_and(n_pos < kf, kf <= n_nonneg)
    tied0 = jnp.where(zero_tie, 1.0, 0.0)

    c = lax.fori_loop(0, FINE_CHECKS[0], fine_body, (jnp.zeros((1, tq), jnp.int32), tied0))
    for lo, hi in zip(FINE_CHECKS, FINE_CHECKS[1:] + (fine_bits,)):
        c = lax.cond(all_done(c), lambda c: c,
                     functools.partial(lax.fori_loop, lo, hi, fine_body), c)
    d = c[0]
    u = jnp.where(zero_tie, jnp.int32(-2 ** 31), u_base + d)
    thr = _ordered_bits_to_float(u)
    thr_next = _ordered_bits_to_float(u + 1)
    next_bits = lax.bitcast_convert_type(thr_next, jnp.int32)
    thr_next = jnp.where(jnp.logical_and(next_bits > 0, next_bits < 0x00800000),
                         float(np.finfo(np.float32).tiny), thr_next)

    @pl.when(jnp.logical_or(jnp.max(tied0) > 0.5, jnp.logical_not(all_done(c))))
    def _():
        need = kf - count(lambda s: s >= thr_next)

        def drop_chunk(j, before):
            k0 = pl.multiple_of(j * cc, cc)
            s_all = sc_ref[pl.ds(k0, cc), :]
            kept = []
            for t in range(cc // tq):
                s = s_all[t * tq:(t + 1) * tq]
                tie = (s >= thr) & (s < thr_next)
                tie01 = jnp.where(tie, 1.0, 0.0)
                within = jnp.dot(tri_ref[...], tie01.astype(bf16), preferred_element_type=f32)
                kept.append(jnp.where(tie & (within >= need - before), -jnp.inf, s))
                before = before + _reduce_rows(jnp.add, tie01, SUBLANES).sum(axis=0, keepdims=True)
            sc_ref[pl.ds(k0, cc), :] = jnp.concatenate(kept, axis=0)
            return before

        lax.fori_loop(0, n_cc, drop_chunk, jnp.zeros((1, tq), f32))

    m_ref[...] = jnp.full(m_ref.shape, NEG_BIG, f32)
    acc_ref[...] = jnp.zeros(acc_ref.shape, f32)

    ones_rows = jnp.ones((SUM_ROWS, ck), bf16)

    def logits_head(j, buf, h):
        k0 = pl.multiple_of(j * ck, ck)
        kp = k_ref[0, pl.ds(k0, ck), (h // 2) * LANES:(h // 2 + 1) * LANES]
        s = jnp.dot(jnp.concatenate([kp, pos_ref[pl.ds(k0, ck), :]], axis=1), qpad_ref[h],
                    preferred_element_type=f32)
        s = s + mb_ref[buf]
        s_ref[buf, h] = s
        mt = _reduce_rows(jnp.maximum, s, SUBLANES).max(axis=0, keepdims=True)
        m_old = m_ref[h:h + 1, :]
        m_new = jnp.maximum(m_old, mt)
        m_ref[h:h + 1, :] = m_new
        stage_ref[buf, 0, h:h + 1, :] = m_new
        stage_ref[buf, 1, h:h + 1, :] = jnp.exp2(m_old - m_new)

    def probs_head(j, buf, h):
        p = jnp.exp2(s_ref[buf, h] - stage_ref[buf, 0, h:h + 1, :]).astype(bf16)
        vt = jnp.concatenate([vT_ref[0, j, h * HEAD_DIM:(h + 1) * HEAD_DIM, :], ones_rows], axis=0)
        pv = jnp.dot(vt, p, preferred_element_type=f32)
        acc_ref[h] = stage_ref[buf, 1, h:h + 1, :] * acc_ref[h] + pv

    def mask_step(j, buf):
        k0 = pl.multiple_of(j * ck, ck)
        mb_ref[buf] = jnp.where(sc_ref[pl.ds(k0, ck), :] >= thr, 0.0, NEG_BIG)

    def logits_step(j, buf):
        mask_step(j, buf)
        for h in range(N_HEADS):
            logits_head(j, buf, h)

    def probs_step(j, buf):
        for h in range(N_HEADS):
            probs_head(j, buf, h)

    def overlapped(jp, bufp, jl, bufl):
        mask_step(jl, bufl)
        for h in range(N_HEADS):
            logits_head(jl, bufl, h)
            probs_head(jp, bufp, h)

    logits_step(0, 0)

    def chunk_pair(t, carry):
        overlapped(2 * t, 0, 2 * t + 1, 1)
        overlapped(2 * t + 1, 1, 2 * t + 2, 0)
        return carry

    n_pairs = (n_ck - 1) // 2
    lax.fori_loop(0, n_pairs, chunk_pair, 0)
    probs_step(2 * n_pairs, 0)

    @pl.when(2 * n_pairs + 1 < n_ck)
    def _():
        logits_step(n_ck - 1, 1)
        probs_step(n_ck - 1, 1)

    outs = []
    for h in range(N_HEADS):
        outs.append(acc_ref[h, 0:HEAD_DIM, :] / acc_ref[h, HEAD_DIM:HEAD_DIM + 1, :])
    o_ref[0] = jnp.concatenate(outs, axis=0).T.astype(bf16)


def _attention(qT, qiT, wT, k, vT, ki, *, topk):
    B, W, S = qT.shape
    tq, ck, cc = ATT_TQ, ATT_CK, ATT_CC
    assert S % cc == 0 and cc == 2 * ck and ck % tq == 0 and vT.shape[3] == ck
    blkT = lambda n: pl.BlockSpec((1, n, tq), lambda b, i: (b, 0, i))
    return pl.pallas_call(
        functools.partial(_attn_kernel, topk=topk),
        out_shape=jax.ShapeDtypeStruct((B, S, W), bf16),
        grid=(B, S // tq),
        in_specs=[blkT(W), blkT(W), blkT(IDX_HEADS),
                  _resident((1, S, W), lambda b, i: (b, 0, 0)),
                  _resident((1, S // ck, W, ck), lambda b, i: (b, 0, 0, 0)),
                  _resident((1, S, LANES), lambda b, i: (b, 0, 0))],
        out_specs=pl.BlockSpec((1, tq, W), lambda b, i: (b, i, 0)),
        scratch_shapes=[pltpu.VMEM((S, tq), f32),
                        pltpu.VMEM((S, tq), coarse_t),
                        pltpu.VMEM((N_HEADS, 2 * HEAD_DIM + LANES, tq), bf16),
                        pltpu.VMEM((IDX_HEADS, 2 * IDX_DIM, tq), bf16),
                        pltpu.VMEM((N_HEADS, tq), f32),
                        pltpu.VMEM((N_HEADS, HEAD_DIM + SUM_ROWS, tq), f32),
                        pltpu.VMEM((S, LANES), bf16),
                        pltpu.VMEM((2, ck, tq), f32),
                        pltpu.VMEM((2, N_HEADS, ck, tq), f32),
                        pltpu.VMEM((2, 2, N_HEADS, tq), f32),
                        pltpu.VMEM((tq, tq), bf16)],
        compiler_params=pltpu.CompilerParams(dimension_semantics=("arbitrary", "arbitrary"),
                                             vmem_limit_bytes=VMEM_LIMIT),
        name="dsa_attn",
    )(qT, qiT, wT, k, vT, ki)


def _post_kernel(x_ref, attn_ref, conv_ref, sga_ref, sgb_ref, gm_ref, shf_ref, scf_ref, gf_ref,
                 gffn_ref, gfin_ref, wap_ref, wcp_ref, wout_ref, wfa_ref, wfb_ref, wfo_ref, o_ref,
                 *, final_norm):
    dot = lambda a, w_ref: jnp.dot(a, w_ref[...], preferred_element_type=f32)
    ya = dot(attn_ref[0], wap_ref)
    yc = dot(conv_ref[0], wcp_ref)
    merged = sga_ref[0].astype(f32) * ya + sgb_ref[0].astype(f32) * yc
    x1 = x_ref[0] + gm_ref[0] * dot(merged.astype(bf16), wout_ref)

    ms = jnp.mean(x1 * x1, axis=-1, keepdims=True)
    h = x1 * lax.rsqrt(ms + EPS) * gffn_ref[...]
    hb = (h * (1.0 + scf_ref[0]) + shf_ref[0]).astype(bf16)
    a = dot(hb, wfa_ref)
    b = dot(hb, wfb_ref)
    act = (a * jax.nn.sigmoid(a) * b).astype(bf16)
    x2 = x1 + gf_ref[0] * dot(act, wfo_ref)

    if final_norm:
        ms2 = jnp.mean(x2 * x2, axis=-1, keepdims=True)
        x2 = x2 * lax.rsqrt(ms2 + EPS) * gfin_ref[...]
    o_ref[0] = x2


def _post(x, attn, conv, sga, sgb, g_m, sh_f, sc_f, g_f, norm_ffn_g, norm_final_g,
          w_attn_proj, w_conv_proj, w_out, w_ffn_in, w_ffn_out, *, tm, final_norm):
    B, S, D = x.shape
    d_ff = w_ffn_out.shape[0]
    weights = [w_attn_proj.astype(bf16), w_conv_proj.astype(bf16), w_out.astype(bf16),
               w_ffn_in[:, :d_ff].astype(bf16), w_ffn_in[:, d_ff:].astype(bf16), w_ffn_out.astype(bf16)]
    tile = lambda n: pl.BlockSpec((1, tm, n), lambda b, i: (b, i, 0))
    vec = pl.BlockSpec((1, 1, D), lambda b, i: (b, 0, 0))
    gain = pl.BlockSpec((1, D), lambda b, i: (0, 0))
    in_specs = [tile(D), tile(ATTN_WIDTH), tile(CONV_CH), tile(D), tile(D), vec, vec, vec, vec, gain, gain]
    in_specs += [_resident(w.shape, lambda b, i: (0, 0)) for w in weights]
    return pl.pallas_call(
        functools.partial(_post_kernel, final_norm=final_norm),
        out_shape=jax.ShapeDtypeStruct((B, S, D), f32),
        grid=(B, S // tm),
        in_specs=in_specs,
        out_specs=tile(D),
        compiler_params=pltpu.CompilerParams(dimension_semantics=("arbitrary", "arbitrary"),
                                             vmem_limit_bytes=VMEM_LIMIT),
        name="post",
    )(x, attn, conv, sga, sgb, g_m, sh_f, sc_f, g_f, norm_ffn_g.reshape(1, D),
      norm_final_g.reshape(1, D), *weights)


def kernel(x, c, norm_mix_g, w_in, w_dw, b_dw, conv_ln_g, conv_ln_b, w_attn_proj, w_conv_proj,
           w_out, norm_ffn_g, w_ffn_in, w_ffn_out, w_ada, b_ada, norm_final_g):
    B, S, D = x.shape
    depth = w_in.shape[0]
    topk = min(TOPK_MAX, S // 4)
    c = c.astype(f32)
    for l in range(depth):
        mod = _ada(c, w_ada[l], b_ada[l])
        sh_m, sc_m, g_m, sh_f, sc_f, g_f = [m.reshape(B, 1, D) for m in jnp.split(mod, N_MOD, axis=-1)]
        qT, qiT, vT, k, ki, wT, conv, sga, sgb = _inproj(
            x, sh_m, sc_m, norm_mix_g[l], w_in[l], w_dw[l], b_dw[l], conv_ln_g[l], conv_ln_b[l],
            tm=512, ck=ATT_CK)
        attn = _attention(qT, qiT, wT, k, vT, ki, topk=topk)
        x = _post(x, attn, conv, sga, sgb, g_m, sh_f, sc_f, g_f, norm_ffn_g[l], norm_final_g,
                  w_attn_proj[l], w_conv_proj[l], w_out[l], w_ffn_in[l], w_ffn_out[l], tm=512,
                  final_norm=(l == depth - 1))
    return x
```
